```python
import math
import jax, jax.numpy as jnp
from jax import lax
import numpy as np

D_MODEL = 1024
BATCH = 16
SEQ = 256
DEPTH = 2
DEC_BATCH = 4
DEC_SEQ = 4096
PAST_LEN = 256

GRID_W = 64
N_MIXERS = 2
N_CONV_LAYERS = (DEPTH + 1) // 2
N_SSD_LAYERS = DEPTH // 2
CONV_W = 3
SSD_EXPAND = 2
D_INNER = SSD_EXPAND * D_MODEL
SSD_HEAD_DIM = 64
SSD_HEADS = D_INNER // SSD_HEAD_DIM
SSD_STATE = 128
SSD_GROUPS = 8
SSD_CHUNK = 128
SSD_CONV_DIM = D_INNER + 2 * SSD_GROUPS * SSD_STATE
SSD_IN_DIM = D_INNER + SSD_CONV_DIM + 2 * SSD_HEADS
N_KEYS = 128
N_EXPERTS = N_KEYS * N_KEYS
PEER_HEADS = 8
PEER_TOPK = 16
PEER_KEY_DIM = 256
PEER_GROUP = 128
EPS = 1e-6

kernel_name = "hybrid_conv_ssd_peer_diffusion_step"


def rmsnorm(x, g):
    xf = x.astype(jnp.float32)
    y = xf * lax.rsqrt(jnp.mean(xf * xf, axis=-1, keepdims=True) + EPS)
    return (y * g.astype(jnp.float32)).astype(x.dtype)


def dwconv3(x, w, grid):
    b, L, C = x.shape
    rows, cols = grid
    xp = jnp.pad(x.reshape(b, rows, cols, C), ((0, 0), (0, 0), (1, 1), (0, 0)))
    y = xp[:, :, :-2] * w[0] + xp[:, :, 1:-1] * w[1] + xp[:, :, 2:] * w[2]
    return y.reshape(b, L, C)


def short_conv_mixer(h, w_in, conv_w, w_out, grid):
    bg, cg, xv = jnp.split(h @ w_in, 3, axis=-1)
    return (bg * dwconv3(cg * xv, conv_w, grid)) @ w_out


def ssd_scan(x, dt, a, bm, cm, h0):
    b, L, H, P = x.shape
    G, N = bm.shape[-2:]
    R = H // G
    Q = SSD_CHUNK
    nc = L // Q
    f32 = jnp.float32
    x = x.astype(f32).reshape(b, nc, Q, G, R, P)
    dt = dt.astype(f32).reshape(b, nc, Q, G, R)
    bm = bm.astype(f32).reshape(b, nc, Q, G, N)
    cm = cm.astype(f32).reshape(b, nc, Q, G, N)
    cum = jnp.cumsum(dt * a.astype(f32).reshape(G, R), axis=2)
    xdt = x * dt[..., None]
    mask = jnp.tril(jnp.ones((Q, Q), dtype=bool))[None, None, :, :, None, None]
    seg = cum[:, :, :, None] - cum[:, :, None, :]
    decay = jnp.exp(jnp.where(mask, seg, -jnp.inf))
    cb = jnp.einsum('bcign,bcjgn->bcijg', cm, bm)
    y_diag = jnp.einsum('bcijgr,bcjgrp->bcigrp', decay * cb[..., None], xdt)
    end_decay = jnp.exp(cum[:, :, -1:] - cum)
    states = jnp.einsum('bcjgn,bcjgrp->bcgrpn', bm, xdt * end_decay[..., None])
    chunk_decay = jnp.exp(cum[:, :, -1])

    def step(h, inp):
        s, d = inp
        return d[..., None, None] * h + s, h

    h_last, h_prev = lax.scan(step, h0.astype(f32).reshape(b, G, R, P, N),
                              (jnp.moveaxis(states, 1, 0), jnp.moveaxis(chunk_decay, 1, 0)))
    h_prev = jnp.moveaxis(h_prev, 0, 1)
    y_off = jnp.einsum('bcign,bcgrpn->bcigrp', cm, h_prev) * jnp.exp(cum)[..., None]
    return (y_diag + y_off).reshape(b, L, H, P), h_last.reshape(b, H, P, N)


def ssd_mixer(h, w_in, conv_w, conv_b, dt_bias, a_log, d_skip, norm_g, w_out, grid, h0):
    b, L, _ = h.shape
    proj = h @ w_in
    z = proj[..., :D_INNER]
    xbc = proj[..., D_INNER:D_INNER + SSD_CONV_DIM]
    dt_raw = proj[..., D_INNER + SSD_CONV_DIM:]
    xbc = jax.nn.silu(dwconv3(xbc, conv_w, grid) + conv_b)
    gn = SSD_GROUPS * SSD_STATE
    xs = xbc[..., :D_INNER].reshape(b, L, SSD_HEADS, SSD_HEAD_DIM)
    bm = xbc[..., D_INNER:D_INNER + gn].reshape(b, L, SSD_GROUPS, SSD_STATE)
    cm = xbc[..., D_INNER + gn:].reshape(b, L, SSD_GROUPS, SSD_STATE)
    dt = jax.nn.softplus(dt_raw.astype(jnp.float32).reshape(b, L, 2, SSD_HEADS)
                         + dt_bias.astype(jnp.float32))
    a = -jnp.exp(a_log.astype(jnp.float32))
    y_f, h_f = ssd_scan(xs, dt[:, :, 0], a[0], bm, cm, h0[:, 0])
    flip = lambda t: jnp.flip(t, axis=1)
    y_b, h_b = ssd_scan(flip(xs), flip(dt[:, :, 1]), a[1], flip(bm), flip(cm), h0[:, 1])
    d_tot = (d_skip[0] + d_skip[1]).astype(jnp.float32)[:, None]
    y = y_f + flip(y_b) + d_tot * xs.astype(jnp.float32)
    y = y.reshape(b, L, D_INNER).astype(h.dtype)
    y = rmsnorm(y * jax.nn.silu(z), norm_g)
    return y @ w_out, jnp.stack([h_f, h_b], axis=1).astype(h.dtype)


def peer(h, wq, keys, u, v):
    b, L, D = h.shape
    T = b * L
    t = h.reshape(T, D)
    q = (t @ wq).reshape(T, PEER_HEADS, 2, PEER_KEY_DIM // 2)
    s = jnp.einsum('thpe,hpne->thpn', q, keys).astype(jnp.float32)
    s1, i1 = lax.top_k(s[:, :, 0], PEER_TOPK)
    s2, i2 = lax.top_k(s[:, :, 1], PEER_TOPK)
    cand = (s1[..., :, None] + s2[..., None, :]).reshape(T, PEER_HEADS, PEER_TOPK * PEER_TOPK)
    cidx = (i1[..., :, None] * N_KEYS + i2[..., None, :]).reshape(T, PEER_HEADS, PEER_TOPK * PEER_TOPK)
    sc, pos = lax.top_k(cand, PEER_TOPK)
    idx = jnp.take_along_axis(cidx, pos, axis=-1)
    g = jax.nn.softmax(sc, axis=-1).astype(h.dtype)
    ng = T // PEER_GROUP

    def group(args):
        tb, ib, gb = args
        act = jnp.einsum('td,thkd->thk', tb, u[ib])
        return jnp.einsum('thk,thkd->td', jax.nn.gelu(act, approximate=False) * gb, v[ib])

    out = lax.map(group, (t.reshape(ng, PEER_GROUP, D),
                          idx.reshape(ng, PEER_GROUP, PEER_HEADS, PEER_TOPK),
                          g.reshape(ng, PEER_GROUP, PEER_HEADS, PEER_TOPK)))
    return out.reshape(b, L, D)


def run_trunk(x, cond, grid, ssm_init, norm_mix_g, norm_ffn_g, norm_f_g, ada_w, ada_b,
              sc_w_in, sc_conv_w, sc_w_out, ssd_w_in, ssd_conv_w, ssd_conv_b, ssd_dt_bias,
              ssd_a_log, ssd_d, ssd_norm_g, ssd_w_out, peer_wq, peer_keys, peer_u, peer_v):
    b = x.shape[0]
    states = []
    for i in range(DEPTH):
        mod = (jax.nn.silu(cond) @ ada_w[i] + ada_b[i])[:, None, :]
        sh_a, scl_a, gt_a, sh_f, scl_f, gt_f = jnp.split(mod, 6, axis=-1)
        hn = rmsnorm(x, norm_mix_g[i]) * (1 + scl_a) + sh_a
        j = i // N_MIXERS
        if i % N_MIXERS == 0:
            mix = short_conv_mixer(hn, sc_w_in[j], sc_conv_w[j], sc_w_out[j], grid)
        else:
            if ssm_init is None:
                h0 = jnp.zeros((b, 2, SSD_HEADS, SSD_HEAD_DIM, SSD_STATE), x.dtype)
            else:
                h0 = ssm_init[:, j]
            mix, st = ssd_mixer(hn, ssd_w_in[j], ssd_conv_w[j], ssd_conv_b[j], ssd_dt_bias[j],
                                ssd_a_log[j], ssd_d[j], ssd_norm_g[j], ssd_w_out[j], grid, h0)
            states.append(st)
        x = x + gt_a * mix
        hn = rmsnorm(x, norm_ffn_g[i]) * (1 + scl_f) + sh_f
        x = x + gt_f * peer(hn, peer_wq[i], peer_keys[i], peer_u[i], peer_v[i])
    return rmsnorm(x, norm_f_g), jnp.stack(states, axis=1)


def setup_inputs(seed: int = 0) -> dict:
    key = jax.random.key(seed)
    ks = iter(jax.random.split(key, 32))
    nrm = lambda shape, scale: jax.random.normal(next(ks), shape, jnp.float32) * scale
    D = D_MODEL
    dt0 = jnp.exp(jax.random.uniform(next(ks), (N_SSD_LAYERS, 2, SSD_HEADS))
                  * (math.log(0.1) - math.log(0.001)) + math.log(0.001))
    return {
        "x_prompt": nrm((BATCH, SEQ, D), 1.0),
        "x_sample": nrm((DEC_BATCH, DEC_SEQ, D), 1.0),
        "state_ssm": nrm((DEC_BATCH, N_SSD_LAYERS, 2, SSD_HEADS, SSD_HEAD_DIM, SSD_STATE), 0.5),
        "c": nrm((DEC_BATCH, D), 1.0),
        "c_ctx": nrm((D,), 1.0),
        "norm_mix_g": 1.0 + nrm((DEPTH, D), 0.02),
        "norm_ffn_g": 1.0 + nrm((DEPTH, D), 0.02),
        "norm_f_g": 1.0 + nrm((D,), 0.02),
        "ada_w": nrm((DEPTH, D, 6 * D), 0.5 * D ** -0.5),
        "ada_b": nrm((DEPTH, 6 * D), 0.02),
        "sc_w_in": nrm((N_CONV_LAYERS, D, 3 * D), D ** -0.5),
        "sc_conv_w": nrm((N_CONV_LAYERS, CONV_W, D), CONV_W ** -0.5),
        "sc_w_out": nrm((N_CONV_LAYERS, D, D), D ** -0.5),
        "ssd_w_in": nrm((N_SSD_LAYERS, D, SSD_IN_DIM), D ** -0.5),
        "ssd_conv_w": nrm((N_SSD_LAYERS, CONV_W, SSD_CONV_DIM), CONV_W ** -0.5),
        "ssd_conv_b": nrm((N_SSD_LAYERS, SSD_CONV_DIM), 0.02),
        "ssd_dt_bias": dt0 + jnp.log(-jnp.expm1(-dt0)),
        "ssd_a_log": jnp.log(jax.random.uniform(next(ks), (N_SSD_LAYERS, 2, SSD_HEADS), jnp.float32, 1.0, 16.0)),
        "ssd_d": 0.5 + nrm((N_SSD_LAYERS, 2, SSD_HEADS), 0.05),
        "ssd_norm_g": 1.0 + nrm((N_SSD_LAYERS, D_INNER), 0.02),
        "ssd_w_out": nrm((N_SSD_LAYERS, D_INNER, D), D_INNER ** -0.5),
        "peer_wq": nrm((DEPTH, D, PEER_HEADS * PEER_KEY_DIM), D ** -0.5),
        "peer_keys": nrm((DEPTH, PEER_HEADS, 2, N_KEYS, PEER_KEY_DIM // 2), (PEER_KEY_DIM // 2) ** -0.5),
        "peer_u": nrm((DEPTH, N_EXPERTS, D), D ** -0.5),
        "peer_v": nrm((DEPTH, N_EXPERTS, D), PEER_HEADS ** -0.5),
    }


def reference(x_prompt, x_sample, state_ssm, c, c_ctx, norm_mix_g, norm_ffn_g, norm_f_g,
              ada_w, ada_b, sc_w_in, sc_conv_w, sc_w_out, ssd_w_in, ssd_conv_w, ssd_conv_b,
              ssd_dt_bias, ssd_a_log, ssd_d, ssd_norm_g, ssd_w_out, peer_wq, peer_keys,
              peer_u, peer_v):
    weights = dict(norm_mix_g=norm_mix_g, norm_ffn_g=norm_ffn_g, norm_f_g=norm_f_g,
                   ada_w=ada_w, ada_b=ada_b, sc_w_in=sc_w_in, sc_conv_w=sc_conv_w,
                   sc_w_out=sc_w_out, ssd_w_in=ssd_w_in, ssd_conv_w=ssd_conv_w,
                   ssd_conv_b=ssd_conv_b, ssd_dt_bias=ssd_dt_bias, ssd_a_log=ssd_a_log,
                   ssd_d=ssd_d, ssd_norm_g=ssd_norm_g, ssd_w_out=ssd_w_out, peer_wq=peer_wq,
                   peer_keys=peer_keys, peer_u=peer_u, peer_v=peer_v)
    ctx_len = x_prompt.shape[1]
    y_prompt, state_ssm_new = run_trunk(x_prompt, c_ctx[None, :], (1, ctx_len), None, **weights)
    rows = x_sample.shape[1] // GRID_W
    y_sample, _ = run_trunk(x_sample, c, (rows, GRID_W), state_ssm, **weights)
    return (y_prompt, y_sample, state_ssm_new)
```

```python
import functools
import math

import jax
import jax.numpy as jnp
from jax import lax
from jax.experimental import pallas as pl
from jax.experimental.pallas import tpu as pltpu

F32 = jnp.float32
BF16 = jnp.bfloat16
EPS = 1e-6

LANES = 128
SUBLANES = 8
VMEM_LIMIT_BYTES = 56 * 1024 * 1024

GRID_W = 64
CONV_W = 3
SSD_HEAD_DIM = 64
SSD_STATE = 128
SSD_GROUPS = 8
SSD_CHUNK = 128
N_KEYS = 128
PEER_HEADS = 8
PEER_TOPK = 16
PEER_HALF = 128

NEG_INF = float("-inf")


def _cparams(sem):
    return pltpu.CompilerParams(dimension_semantics=sem, vmem_limit_bytes=VMEM_LIMIT_BYTES)


def _norm_mod(x, g, scl, sh):
    ms = jnp.mean(x * x, axis=-1, keepdims=True)
    return x * lax.rsqrt(ms + EPS) * g * (1.0 + scl) + sh


def _silu(x):
    return x * (1.0 / (1.0 + jnp.exp(-x)))


def _mod_row_map(layer, n_ctx_tiles, tiles_per_lat_seq):
    def index_map(i, *_):
        row = jnp.where(i < n_ctx_tiles, 0, 1 + (i - n_ctx_tiles) // tiles_per_lat_seq)
        return (layer, row, 0, 0)
    return index_map


def _shift_rows(u, pos, rowlen):
    n = u.shape[0]
    up = pltpu.roll(u, 1, axis=0)
    dn = pltpu.roll(u, n - 1, axis=0)
    up = jnp.where(pos == 0, 0.0, up)
    dn = jnp.where(pos == rowlen - 1, 0.0, dn)
    return up, dn


def _adaln_kernel(c_ref, w_ref, b_ref, o_ref):
    s = _silu(c_ref[...])
    o_ref[...] = jnp.dot(s, w_ref[...], preferred_element_type=F32,
                         precision=lax.Precision.HIGHEST) + b_ref[...]


def _adaln(cond, ada_w, ada_b):
    depth, d, n = ada_w.shape
    rows = cond.shape[0]
    tn = 1536
    return pl.pallas_call(
        _adaln_kernel,
        grid=(depth, n // tn),
        in_specs=[
            pl.BlockSpec((rows, d), lambda l, j: (0, 0)),
            pl.BlockSpec((None, d, tn), lambda l, j: (l, 0, j)),
            pl.BlockSpec((None, 1, tn), lambda l, j: (l, 0, j)),
        ],
        out_specs=pl.BlockSpec((None, rows, tn), lambda l, j: (l, 0, j)),
        out_shape=jax.ShapeDtypeStruct((depth, rows, n), F32),
        compiler_params=_cparams(("arbitrary", "arbitrary")),
        name="adaln",
    )(cond, ada_w, ada_b.reshape(depth, 1, n))


def _conv_mixer_kernel(n_ctx_tiles, ctx_rowlen, x_ref, mod_ref, g_ref, win_ref, cw_ref,
                       wout_ref, o_ref):
    i = pl.program_id(0)
    x = x_ref[...]
    d = x.shape[1]
    mod = mod_ref[...]
    hn = _norm_mod(x, g_ref[...], mod[1:2], mod[0:1]).astype(BF16)
    p = jnp.dot(hn, win_ref[...], preferred_element_type=F32)
    bg, cg, xv = p[:, :d], p[:, d:2 * d], p[:, 2 * d:]
    u = cg * xv
    rowlen = jnp.where(i < n_ctx_tiles, ctx_rowlen, GRID_W)
    pos = lax.broadcasted_iota(jnp.int32, (x.shape[0], 1), 0) & (rowlen - 1)
    up, dn = _shift_rows(u, pos, rowlen)
    cw = cw_ref[...]
    y = up * cw[0:1] + u * cw[1:2] + dn * cw[2:3]
    mix = jnp.dot((bg * y).astype(BF16), wout_ref[...], preferred_element_type=F32)
    o_ref[...] = x + mod[2:3] * mix


def _conv_mixer(x, mod4, layer, g, w_in, conv_w, w_out, *, tt, n_ctx_tiles, ctx_rowlen,
                tiles_per_lat_seq):
    t, d = x.shape
    return pl.pallas_call(
        functools.partial(_conv_mixer_kernel, n_ctx_tiles, ctx_rowlen),
        grid=(t // tt,),
        in_specs=[
            pl.BlockSpec((tt, d), lambda i: (i, 0)),
            pl.BlockSpec((None, None, 6, d), _mod_row_map(layer, n_ctx_tiles, tiles_per_lat_seq)),
            pl.BlockSpec((1, d), lambda i: (0, 0)),
            pl.BlockSpec((d, 3 * d), lambda i: (0, 0)),
            pl.BlockSpec((CONV_W, d), lambda i: (0, 0)),
            pl.BlockSpec((d, d), lambda i: (0, 0)),
        ],
        out_specs=pl.BlockSpec((tt, d), lambda i: (i, 0)),
        out_shape=jax.ShapeDtypeStruct((t, d), F32),
        compiler_params=_cparams(("arbitrary",)),
        name="conv_mixer",
    )(x, mod4, g, w_in, conv_w, w_out)


def _top16_rows(s, ids):
    big = jnp.float32(1e9)
    vals, sel_ids = [], []
    for _ in range(PEER_TOPK):
        m = jnp.max(s, axis=0, keepdims=True)
        sel = jnp.min(jnp.where(s == m, ids, big), axis=0, keepdims=True)
        s = jnp.where(ids == sel, NEG_INF, s)
        vals.append(m)
        sel_ids.append(sel)
    return jnp.concatenate(vals, axis=0), jnp.concatenate(sel_ids, axis=0)


def _gather_rows16(table, k):
    rows = lax.broadcasted_iota(jnp.int32, (PEER_TOPK, 1), 0).astype(F32)
    return jnp.sum(jnp.where(rows == k, table, 0.0), axis=0, keepdims=True)


def _peer_route_kernel(x_ref, mod_ref, g_ref, wq_ref, keys_ref, i1_ref, i2_ref, gate_ref):
    x = x_ref[...]
    mod = mod_ref[...]
    hn = _norm_mod(x, g_ref[...], mod[4:5], mod[3:4]).astype(BF16)
    q = jnp.dot(hn, wq_ref[...], preferred_element_type=F32).astype(BF16)

    key_ids = lax.broadcasted_iota(jnp.int32, (N_KEYS, 1), 0).astype(F32)
    n_cand = PEER_TOPK * SUBLANES + SUBLANES
    r = lax.broadcasted_iota(jnp.int32, (n_cand, 1), 0)
    flat = jnp.where(r < PEER_TOPK * SUBLANES, r + ((r >> 3) << 3),
                     r - (PEER_TOPK * SUBLANES - SUBLANES)).astype(F32)

    i1_all, i2_all, gate_all = [], [], []
    for h in range(PEER_HEADS):
        tops = []
        for p in range(2):
            hp = 2 * h + p
            st = lax.dot_general(keys_ref[hp], q[:, hp * PEER_HALF:(hp + 1) * PEER_HALF],
                                 (((1,), (1,)), ((), ())), preferred_element_type=F32)
            tops.append(_top16_rows(st, key_ids))
        (s1, k1), (s2, k2) = tops
        cands = [s1[a:a + 1] + s2[0:SUBLANES] for a in range(PEER_TOPK)]
        cands.append(s1[0:1] + s2[SUBLANES:])
        sc, pos = _top16_rows(jnp.concatenate(cands, axis=0), flat)
        pos1 = jnp.floor(pos * (1.0 / PEER_TOPK))
        pos2 = pos - pos1 * PEER_TOPK
        e1 = jnp.concatenate([_gather_rows16(k1, pos1[a:a + 1]) for a in range(PEER_TOPK)], axis=0)
        e2 = jnp.concatenate([_gather_rows16(k2, pos2[a:a + 1]) for a in range(PEER_TOPK)], axis=0)
        ex = jnp.exp(sc - sc[0:1])
        gate = ex / jnp.sum(ex, axis=0, keepdims=True)
        i1_all.append(e1)
        i2_all.append(e2)
        gate_all.append(gate)
    i1_ref[...] = jnp.concatenate(i1_all, axis=0).T
    i2_ref[...] = jnp.concatenate(i2_all, axis=0).T
    gate_ref[...] = jnp.concatenate(gate_all, axis=0).T


def _peer_route(x, mod4, layer, g, wq, keys, *, tt, n_ctx_tiles, tiles_per_lat_seq):
    t, d = x.shape
    nsel = PEER_HEADS * PEER_TOPK
    out = jax.ShapeDtypeStruct((t, nsel), F32)
    return pl.pallas_call(
        _peer_route_kernel,
        grid=(t // tt,),
        in_specs=[
            pl.BlockSpec((tt, d), lambda i: (i, 0)),
            pl.BlockSpec((None, None, 6, d), _mod_row_map(layer, n_ctx_tiles, tiles_per_lat_seq)),
            pl.BlockSpec((1, d), lambda i: (0, 0)),
            pl.BlockSpec(wq.shape, lambda i: (0, 0)),
            pl.BlockSpec(keys.shape, lambda i: (0, 0, 0)),
        ],
        out_specs=[pl.BlockSpec((tt, nsel), lambda i: (i, 0))] * 3,
        out_shape=[out, out, out],
        compiler_params=_cparams(("arbitrary",)),
        name="peer_route",
    )(x, mod4, g, wq, keys)


def _gate_pitch(tt):
    return tt + SUBLANES


def _peer_dense_kernel(final_norm, tt, keys_per_blk, x_ref, mod_ref, g_ref, i1_ref, i2_ref,
                       gate_ref, u_ref, v_ref, gf_ref, o_ref, hn_s, w_s, acc_s):
    eb = pl.program_id(1)
    pitch = _gate_pitch(tt)

    @pl.when(eb == 0)
    def _():
        mod = mod_ref[...]
        hn_s[...] = _norm_mod(x_ref[...], g_ref[...], mod[4:5], mod[3:4]).astype(BF16)
        acc_s[...] = jnp.zeros_like(acc_s)
        key_ids = lax.broadcasted_iota(jnp.int32, (N_KEYS, N_KEYS), 0).astype(F32)

        def build(t, carry):
            i1 = i1_ref[pl.ds(t, 1), :]
            i2 = i2_ref[pl.ds(t, 1), :]
            gt = gate_ref[pl.ds(t, 1), :]
            p = jnp.where(i1 == key_ids, gt, 0.0).astype(BF16)
            q = jnp.where(i2 == key_ids, 1.0, 0.0).astype(BF16)
            w = lax.dot_general(p, q, (((1,), (1,)), ((), ())), preferred_element_type=F32)
            w_s[pl.ds(t, N_KEYS, stride=pitch), :] = w
            return carry

        lax.fori_loop(0, tt, build, 0, unroll=8)

    act = lax.dot_general(hn_s[...], u_ref[...], (((1,), (1,)), ((), ())),
                          preferred_element_type=F32)
    hs = []
    for a in range(keys_per_blk):
        start = pl.multiple_of((eb * keys_per_blk + a) * pitch, SUBLANES)
        wa = w_s[pl.ds(start, tt), :]
        xa = act[:, a * N_KEYS:(a + 1) * N_KEYS]
        gelu = 0.5 * xa * (1.0 + lax.erf(xa * math.sqrt(0.5)))
        hs.append((gelu * wa).astype(BF16))
    h = jnp.concatenate(hs, axis=1)
    acc_s[...] += jnp.dot(h, v_ref[...], preferred_element_type=F32)

    @pl.when(eb == pl.num_programs(1) - 1)
    def _():
        y = x_ref[...] + mod_ref[...][5:6] * acc_s[...]
        if final_norm:
            ms = jnp.mean(y * y, axis=-1, keepdims=True)
            y = y * lax.rsqrt(ms + EPS) * gf_ref[...]
        o_ref[...] = y


def _peer_dense(x, mod4, layer, g, i1, i2, gate, u, v, g_final, *, tt, keys_per_blk,
                n_ctx_tiles, tiles_per_lat_seq, final_norm):
    t, d = x.shape
    n_exp = u.shape[0]
    eblk = keys_per_blk * N_KEYS
    nsel = i1.shape[1]
    pitch = _gate_pitch(tt)
    mod_map = _mod_row_map(layer, n_ctx_tiles, tiles_per_lat_seq)
    return pl.pallas_call(
        functools.partial(_peer_dense_kernel, final_norm, tt, keys_per_blk),
        grid=(t // tt, n_exp // eblk),
        in_specs=[
            pl.BlockSpec((tt, d), lambda i, e: (i, 0)),
            pl.BlockSpec((None, None, 6, d), mod_map),
            pl.BlockSpec((1, d), lambda i, e: (0, 0)),
            pl.BlockSpec((tt, nsel), lambda i, e: (i, 0)),
            pl.BlockSpec((tt, nsel), lambda i, e: (i, 0)),
            pl.BlockSpec((tt, nsel), lambda i, e: (i, 0)),
            pl.BlockSpec((eblk, d), lambda i, e: (e, 0)),
            pl.BlockSpec((eblk, d), lambda i, e: (e, 0)),
            pl.BlockSpec((1, d), lambda i, e: (0, 0)),
        ],
        out_specs=pl.BlockSpec((tt, d), lambda i, e: (i, 0)),
        out_shape=jax.ShapeDtypeStruct((t, d), F32),
        scratch_shapes=[
            pltpu.VMEM((tt, d), BF16),
            pltpu.VMEM((N_KEYS * pitch, N_KEYS), F32),
            pltpu.VMEM((tt, d), F32),
        ],
        compiler_params=_cparams(("arbitrary", "arbitrary")),
        name="peer_dense",
    )(x, mod4, g, i1, i2, gate, u, v, g_final)


def _softplus(x):
    return jnp.maximum(x, 0.0) + jnp.log(1.0 + jnp.exp(-jnp.abs(x)))


def _ssd_in_kernel(n_ctx_tiles, ctx_rowlen, x_ref, mod_ref, g_ref, wz_ref, wx_ref, wdt_ref,
                   cw_ref, cb_ref, dtb_ref, z_ref, xbc_ref, dt_ref):
    i = pl.program_id(0)
    x = x_ref[...]
    mod = mod_ref[...]
    hn = _norm_mod(x, g_ref[...], mod[1:2], mod[0:1]).astype(BF16)
    z_ref[...] = jnp.dot(hn, wz_ref[...], preferred_element_type=F32)
    xbc = jnp.dot(hn, wx_ref[...], preferred_element_type=F32)
    rowlen = jnp.where(i < n_ctx_tiles, ctx_rowlen, GRID_W)
    pos = lax.broadcasted_iota(jnp.int32, (x.shape[0], 1), 0) & (rowlen - 1)
    up, dn = _shift_rows(xbc, pos, rowlen)
    cw = cw_ref[...]
    xbc_ref[...] = _silu(up * cw[0:1] + xbc * cw[1:2] + dn * cw[2:3] + cb_ref[...])
    dt = jnp.dot(hn, wdt_ref[...], preferred_element_type=F32)
    dtb = dtb_ref[...]
    dt_ref[0] = _softplus(dt[:, :LANES] + dtb[0:1])
    dt_ref[1] = _softplus(dt[:, LANES:] + dtb[1:2])


def _ssd_in(x, mod4, layer, g, w_z, w_x, w_dt, conv_w, conv_b, dt_bias, *, tt, n_ctx_tiles,
            ctx_rowlen, tiles_per_lat_seq):
    t, d = x.shape
    dz, dx = w_z.shape[1], w_x.shape[1]
    const = lambda i: (0, 0)
    return pl.pallas_call(
        functools.partial(_ssd_in_kernel, n_ctx_tiles, ctx_rowlen),
        grid=(t // tt,),
        in_specs=[
            pl.BlockSpec((tt, d), lambda i: (i, 0)),
            pl.BlockSpec((None, None, 6, d), _mod_row_map(layer, n_ctx_tiles, tiles_per_lat_seq)),
            pl.BlockSpec((1, d), const),
            pl.BlockSpec((d, dz), const),
            pl.BlockSpec((d, dx), const),
            pl.BlockSpec((d, 2 * LANES), const),
            pl.BlockSpec((CONV_W, dx), const),
            pl.BlockSpec((1, dx), const),
            pl.BlockSpec((2, LANES), const),
        ],
        out_specs=[
            pl.BlockSpec((tt, dz), lambda i: (i, 0)),
            pl.BlockSpec((tt, dx), lambda i: (i, 0)),
            pl.BlockSpec((2, tt, LANES), lambda i: (0, i, 0)),
        ],
        out_shape=[
            jax.ShapeDtypeStruct((t, dz), F32),
            jax.ShapeDtypeStruct((t, dx), F32),
            jax.ShapeDtypeStruct((2, t, LANES), F32),
        ],
        compiler_params=_cparams(("arbitrary",)),
        name="ssd_in",
    )(x, mod4, g, w_z, w_x, w_dt, conv_w, conv_b, dt_bias)


def _ssd_scan_kernel(has_h0, has_prev, emit_state, n_heads, *refs):
    refs = list(refs)
    xbc_ref, dt_ref, a_ref = refs[:3]
    refs = refs[3:]
    h0_ref = refs.pop(0) if has_h0 else None
    if has_prev:
        refs.pop(0)
    y_ref = refs.pop(0)
    st_ref = refs.pop(0) if emit_state else None
    (state_s,) = refs
    direction = pl.program_id(1)
    c = pl.program_id(2)
    q = SSD_CHUNK
    p = SSD_HEAD_DIM
    d_inner = n_heads * p
    gn = SSD_GROUPS * SSD_STATE
    heads_per_group = n_heads // SSD_GROUPS

    @pl.when(c == 0)
    def _():
        if has_h0:
            state_s[...] = h0_ref[...]
        else:
            state_s[...] = jnp.zeros_like(state_s)

    row = lax.broadcasted_iota(jnp.int32, (q, q), 0)
    col = lax.broadcasted_iota(jnp.int32, (q, q), 1)
    fwd = direction == 0
    mask = (col - row) * jnp.where(fwd, 1, -1) <= 0
    tri = jnp.where(mask, 1.0, 0.0)

    dt = dt_ref[...]
    dta = dt * (-jnp.exp(a_ref[...]))
    cum = jnp.dot(tri, dta, preferred_element_type=F32, precision=lax.Precision.HIGHEST)
    cum_t = cum.T
    tot = jnp.where(fwd, cum[q - 1:q, :], cum[0:1, :])
    exp_cum = jnp.exp(cum)
    dt_end = dt * jnp.exp(tot - cum)
    exp_tot = jnp.exp(tot)

    lane = lax.broadcasted_iota(jnp.int32, (q, 2 * p), 1)
    sub = lax.broadcasted_iota(jnp.int32, (2 * p, 1), 0)
    first = lane < p

    def pair_cols(mat, hd):
        return jnp.where(first, mat[:, hd:hd + 1], mat[:, hd + 1:hd + 2])

    for g in range(SSD_GROUPS):
        bm = xbc_ref[:, d_inner + g * SSD_STATE:d_inner + (g + 1) * SSD_STATE].astype(BF16)
        cm = xbc_ref[:, d_inner + gn + g * SSD_STATE:d_inner + gn + (g + 1) * SSD_STATE].astype(BF16)
        cb = lax.dot_general(cm, bm, (((1,), (1,)), ((), ())), preferred_element_type=F32)
        for pr in range(heads_per_group // 2):
            hd = g * heads_per_group + 2 * pr
            pair = hd // 2
            xs = xbc_ref[:, hd * p:(hd + 2) * p]
            xdt = xs * pair_cols(dt, hd)
            y = None
            for k in range(2):
                seg = cum[:, hd + k:hd + k + 1] - cum_t[hd + k:hd + k + 1, :]
                m = (jnp.where(mask, jnp.exp(seg), 0.0) * cb).astype(BF16)
                rhs = jnp.where(first if k == 0 else jnp.logical_not(first), xdt, 0.0).astype(BF16)
                yk = jnp.dot(m, rhs, preferred_element_type=F32)
                y = yk if y is None else y + yk
            h_prev = state_s[pair]
            y_off = lax.dot_general(cm, h_prev.astype(BF16), (((1,), (1,)), ((), ())),
                                    preferred_element_type=F32)
            y_ref[:, hd * p:(hd + 2) * p] = y + y_off * pair_cols(exp_cum, hd)
            xdtw = (xs * pair_cols(dt_end, hd)).astype(BF16)
            upd = lax.dot_general(xdtw, bm, (((0,), (0,)), ((), ())), preferred_element_type=F32)
            dec = jnp.where(sub < p, exp_tot[:, hd:hd + 1], exp_tot[:, hd + 1:hd + 2])
            state_s[pair] = dec * h_prev + upd

    if emit_state:
        @pl.when(c == pl.num_programs(2) - 1)
        def _():
            st_ref[...] = state_s[...]


def _ssd_scan(xbc, dt, a_log, h0, y_prev, *, n_seq, n_chunks, chunk_off, n_heads, emit_state):
    t, dx = xbc.shape
    d_inner = n_heads * SSD_HEAD_DIM
    n_pairs = n_heads // 2
    q = SSD_CHUNK

    def chunk_of(s, d, c):
        return chunk_off + s * n_chunks + jnp.where(d == 0, c, n_chunks - 1 - c)

    in_specs = [
        pl.BlockSpec((q, dx), lambda s, d, c: (chunk_of(s, d, c), 0)),
        pl.BlockSpec((None, q, LANES), lambda s, d, c: (d, chunk_of(s, d, c), 0)),
        pl.BlockSpec((None, 1, LANES), lambda s, d, c: (d, 0, 0)),
    ]
    args = [xbc, dt, a_log]
    if h0 is not None:
        in_specs.append(pl.BlockSpec((None, None, n_pairs, 2 * SSD_HEAD_DIM, SSD_STATE),
                                     lambda s, d, c: (s, d, 0, 0, 0)))
        args.append(h0)
    y_shape = jax.ShapeDtypeStruct((2, t, d_inner), F32)
    aliases = {}
    if y_prev is not None:
        aliases = {len(args): 0}
        in_specs.append(pl.BlockSpec(memory_space=pl.ANY))
        args.append(y_prev)
    out_specs = [pl.BlockSpec((None, q, d_inner), lambda s, d, c: (d, chunk_of(s, d, c), 0))]
    out_shape = [y_shape]
    if emit_state:
        out_specs.append(pl.BlockSpec((None, None, n_pairs, 2 * SSD_HEAD_DIM, SSD_STATE),
                                      lambda s, d, c: (s, d, 0, 0, 0)))
        out_shape.append(jax.ShapeDtypeStruct((n_seq, 2, n_pairs, 2 * SSD_HEAD_DIM, SSD_STATE), F32))
    return pl.pallas_call(
        functools.partial(_ssd_scan_kernel, h0 is not None, y_prev is not None, emit_state, n_heads),
        grid=(n_seq, 2, n_chunks),
        in_specs=in_specs,
        out_specs=out_specs,
        out_shape=out_shape,
        scratch_shapes=[pltpu.VMEM((n_pairs, 2 * SSD_HEAD_DIM, SSD_STATE), F32)],
        input_output_aliases=aliases,
        compiler_params=_cparams(("arbitrary", "arbitrary", "arbitrary")),
        name="ssd_scan",
    )(*args)


def _ssd_out_kernel(x_ref, mod_ref, y_ref, xs_ref, z_ref, dskip_ref, ng_ref, wout_ref, o_ref):
    y = y_ref[0] + y_ref[1] + dskip_ref[...] * xs_ref[...]
    yz = y * _silu(z_ref[...])
    ms = jnp.mean(yz * yz, axis=-1, keepdims=True)
    yn = (yz * lax.rsqrt(ms + EPS) * ng_ref[...]).astype(BF16)
    mix = jnp.dot(yn, wout_ref[...], preferred_element_type=F32)
    o_ref[...] = x_ref[...] + mod_ref[...][2:3] * mix


def _ssd_out(x, mod4, layer, y, xbc, z, d_skip, norm_g, w_out, *, tt, n_ctx_tiles,
             tiles_per_lat_seq):
    t, d = x.shape
    di = z.shape[1]
    const = lambda i: (0, 0)
    return pl.pallas_call(
        _ssd_out_kernel,
        grid=(t // tt,),
        in_specs=[
            pl.BlockSpec((tt, d), lambda i: (i, 0)),
            pl.BlockSpec((None, None, 6, d), _mod_row_map(layer, n_ctx_tiles, tiles_per_lat_seq)),
            pl.BlockSpec((2, tt, di), lambda i: (0, i, 0)),
            pl.BlockSpec((tt, di), lambda i: (i, 0)),
            pl.BlockSpec((tt, di), lambda i: (i, 0)),
            pl.BlockSpec((1, di), const),
            pl.BlockSpec((1, di), const),
            pl.BlockSpec((di, d), const),
        ],
        out_specs=pl.BlockSpec((tt, d), lambda i: (i, 0)),
        out_shape=jax.ShapeDtypeStruct((t, d), F32),
        compiler_params=_cparams(("arbitrary",)),
        name="ssd_out",
    )(x, mod4, y, xbc, z, d_skip, norm_g, w_out)


def _pick_tile(candidates, *lengths):
    for tt in candidates:
        if all(n % tt == 0 for n in lengths):
            return tt
    raise ValueError(f"no tile in {candidates} divides {lengths}")


def kernel(x_prompt, x_sample, state_ssm, c, c_ctx, norm_mix_g, norm_ffn_g, norm_f_g, ada_w, ada_b, sc_w_in, sc_conv_w, sc_w_out, ssd_w_in, ssd_conv_w, ssd_conv_b, ssd_dt_bias, ssd_a_log, ssd_d, ssd_norm_g, ssd_w_out, peer_wq, peer_keys, peer_u, peer_v):
    n_ctx_seq, ctx_len, d = x_prompt.shape
    n_lat_seq, lat_len, _ = x_sample.shape
    depth = ada_w.shape[0]
    t_ctx = n_ctx_seq * ctx_len
    d_inner = ssd_norm_g.shape[1]
    n_heads = d_inner // SSD_HEAD_DIM
    conv_dim = ssd_conv_w.shape[2]
    assert ctx_len & (ctx_len - 1) == 0 and ctx_len % SSD_CHUNK == 0 and lat_len % SSD_CHUNK == 0
    assert n_heads % (2 * SSD_GROUPS) == 0 and n_heads <= LANES

    x = jnp.concatenate([x_prompt.reshape(t_ctx, d), x_sample.reshape(n_lat_seq * lat_len, d)], axis=0)
    n_cond = 1 + n_lat_seq
    cond_rows = -(-n_cond // SUBLANES) * SUBLANES
    cond = jnp.concatenate([c_ctx[None], c, jnp.zeros((cond_rows - n_cond, d), F32)], axis=0)
    mod4 = _adaln(cond, ada_w, ada_b).reshape(depth, cond_rows, 6, d)

    def tiling(tt):
        return dict(tt=tt, n_ctx_tiles=t_ctx // tt, tiles_per_lat_seq=lat_len // tt)

    tt_mix = _pick_tile((512, 256), t_ctx, lat_len)
    tt_route = _pick_tile((256, 128), t_ctx, lat_len)
    tt_dense = _pick_tile((512, 256), t_ctx, lat_len)
    assert tt_mix % ctx_len == 0 and tt_mix % GRID_W == 0

    states = []
    for i in range(depth):
        j = i // 2
        if i % 2 == 0:
            x = _conv_mixer(x, mod4, i, norm_mix_g[i][None], sc_w_in[j].astype(BF16), sc_conv_w[j],
                            sc_w_out[j].astype(BF16), ctx_rowlen=ctx_len, **tiling(tt_mix))
        else:
            w_in = ssd_w_in[j]
            w_z = w_in[:, :d_inner].astype(BF16)
            w_x = w_in[:, d_inner:d_inner + conv_dim].astype(BF16)
            w_dt_raw = w_in[:, d_inner + conv_dim:]
            pad = jnp.zeros((d, LANES - n_heads), F32)
            w_dt = jnp.concatenate([w_dt_raw[:, :n_heads], pad, w_dt_raw[:, n_heads:], pad], axis=1).astype(BF16)
            lane_pad = ((0, 0), (0, LANES - n_heads))
            dt_bias = jnp.pad(ssd_dt_bias[j], lane_pad)
            a_log = jnp.pad(ssd_a_log[j], lane_pad)[:, None, :]
            z, xbc, dt = _ssd_in(x, mod4, i, norm_mix_g[i][None], w_z, w_x, w_dt, ssd_conv_w[j],
                                 ssd_conv_b[j][None], dt_bias, ctx_rowlen=ctx_len, **tiling(tt_mix))
            n_pairs = n_heads // 2
            y, st = _ssd_scan(xbc, dt, a_log, None, None, n_seq=n_ctx_seq,
                              n_chunks=ctx_len // SSD_CHUNK, chunk_off=0, n_heads=n_heads,
                              emit_state=True)
            h0 = state_ssm[:, j].reshape(n_lat_seq, 2, n_pairs, 2 * SSD_HEAD_DIM, SSD_STATE)
            (y,) = _ssd_scan(xbc, dt, a_log, h0, y, n_seq=n_lat_seq,
                             n_chunks=lat_len // SSD_CHUNK, chunk_off=t_ctx // SSD_CHUNK,
                             n_heads=n_heads, emit_state=False)
            states.append(st.reshape(n_ctx_seq, 2, n_heads, SSD_HEAD_DIM, SSD_STATE))
            d_skip = jnp.repeat(ssd_d[j][0] + ssd_d[j][1], SSD_HEAD_DIM)[None]
            x = _ssd_out(x, mod4, i, y, xbc, z, d_skip, ssd_norm_g[j][None],
                         ssd_w_out[j].astype(BF16), **tiling(tt_mix))
        keys = peer_keys[i].reshape(2 * PEER_HEADS, N_KEYS, PEER_HALF).astype(BF16)
        i1, i2, gate = _peer_route(x, mod4, i, norm_ffn_g[i][None], peer_wq[i].astype(BF16), keys,
                                   **tiling(tt_route))
        x = _peer_dense(x, mod4, i, norm_ffn_g[i][None], i1, i2, gate, peer_u[i].astype(BF16),
                        peer_v[i].astype(BF16), norm_f_g[None], keys_per_blk=4,
                        final_norm=(i == depth - 1), **tiling(tt_dense))

    y_prompt = x[:t_ctx].reshape(n_ctx_seq, ctx_len, d)
    y_sample = x[t_ctx:].reshape(n_lat_seq, lat_len, d)
    return y_prompt, y_sample, jnp.stack(states, axis=1)
```

```python
import functools
import math

import jax
import jax.numpy as jnp
from jax import lax
from jax.experimental import pallas as pl
from jax.experimental.pallas import tpu as pltpu

F32 = jnp.float32
BF16 = jnp.bfloat16
EPS = 1e-6

LANES = 128
SUBLANES = 8
VMEM_LIMIT_BYTES = 56 * 1024 * 1024

GRID_W = 64
CONV_W = 3
SSD_HEAD_DIM = 64
SSD_STATE = 128
SSD_GROUPS = 8
SSD_CHUNK = 128
N_KEYS = 128
PEER_HEADS = 8
PEER_TOPK = 16
PEER_HALF = 128

NEG_INF = float("-inf")


def _cparams(sem):
    return pltpu.CompilerParams(dimension_semantics=sem, vmem_limit_bytes=VMEM_LIMIT_BYTES)


def _norm_mod(x, g, scl, sh):
    ms = jnp.mean(x * x, axis=-1, keepdims=True)
    return x * lax.rsqrt(ms + EPS) * g * (1.0 + scl) + sh


def _silu(x):
    return x * (1.0 / (1.0 + jnp.exp(-x)))


def _mod_row_map(layer, n_ctx_tiles, tiles_per_lat_seq):
    def index_map(i, *_):
        row = jnp.where(i < n_ctx_tiles, 0, 1 + (i - n_ctx_tiles) // tiles_per_lat_seq)
        return (layer, row, 0, 0)
    return index_map


def _shift_rows(u, pos, rowlen):
    n = u.shape[0]
    up = pltpu.roll(u, 1, axis=0)
    dn = pltpu.roll(u, n - 1, axis=0)
    up = jnp.where(pos == 0, 0.0, up)
    dn = jnp.where(pos == rowlen - 1, 0.0, dn)
    return up, dn


def _adaln_kernel(c_ref, w_ref, b_ref, o_ref):
    s = _silu(c_ref[...])
    o_ref[...] = jnp.dot(s, w_ref[...], preferred_element_type=F32,
                         precision=lax.Precision.HIGHEST) + b_ref[...]


def _adaln(cond, ada_w, ada_b):
    depth, d, n = ada_w.shape
    rows = cond.shape[0]
    tn = 1536
    return pl.pallas_call(
        _adaln_kernel,
        grid=(depth, n // tn),
        in_specs=[
            pl.BlockSpec((rows, d), lambda l, j: (0, 0)),
            pl.BlockSpec((None, d, tn), lambda l, j: (l, 0, j)),
            pl.BlockSpec((None, 1, tn), lambda l, j: (l, 0, j)),
        ],
        out_specs=pl.BlockSpec((None, rows, tn), lambda l, j: (l, 0, j)),
        out_shape=jax.ShapeDtypeStruct((depth, rows, n), F32),
        compiler_params=_cparams(("arbitrary", "arbitrary")),
        name="adaln",
    )(cond, ada_w, ada_b.reshape(depth, 1, n))


def _conv_mixer_kernel(n_ctx_tiles, ctx_rowlen, x_ref, mod_ref, g_ref, win_ref, cw_ref,
                       wout_ref, o_ref):
    i = pl.program_id(0)
    x = x_ref[...]
    d = x.shape[1]
    mod = mod_ref[...]
    hn = _norm_mod(x, g_ref[...], mod[1:2], mod[0:1]).astype(BF16)
    p = jnp.dot(hn, win_ref[...], preferred_element_type=F32)
    bg, cg, xv = p[:, :d], p[:, d:2 * d], p[:, 2 * d:]
    u = cg * xv
    rowlen = jnp.where(i < n_ctx_tiles, ctx_rowlen, GRID_W)
    pos = lax.broadcasted_iota(jnp.int32, (x.shape[0], 1), 0) & (rowlen - 1)
    up, dn = _shift_rows(u, pos, rowlen)
    cw = cw_ref[...]
    y = up * cw[0:1] + u * cw[1:2] + dn * cw[2:3]
    mix = jnp.dot((bg * y).astype(BF16), wout_ref[...], preferred_element_type=F32)
    o_ref[...] = x + mod[2:3] * mix


def _conv_mixer(x, mod4, layer, g, w_in, conv_w, w_out, *, tt, n_ctx_tiles, ctx_rowlen,
                tiles_per_lat_seq):
    t, d = x.shape
    return pl.pallas_call(
        functools.partial(_conv_mixer_kernel, n_ctx_tiles, ctx_rowlen),
        grid=(t // tt,),
        in_specs=[
            pl.BlockSpec((tt, d), lambda i: (i, 0)),
            pl.BlockSpec((None, None, 6, d), _mod_row_map(layer, n_ctx_tiles, tiles_per_lat_seq)),
            pl.BlockSpec((1, d), lambda i: (0, 0)),
            pl.BlockSpec((d, 3 * d), lambda i: (0, 0)),
            pl.BlockSpec((CONV_W, d), lambda i: (0, 0)),
            pl.BlockSpec((d, d), lambda i: (0, 0)),
        ],
        out_specs=pl.BlockSpec((tt, d), lambda i: (i, 0)),
        out_shape=jax.ShapeDtypeStruct((t, d), F32),
        compiler_params=_cparams(("arbitrary",)),
        name="conv_mixer",
    )(x, mod4, g, w_in, conv_w, w_out)


def _top16_rows(s, ids):
    big = jnp.float32(1e9)
    vals, sel_ids = [], []
    for _ in range(PEER_TOPK):
        m = jnp.max(s, axis=0, keepdims=True)
        sel = jnp.min(jnp.where(s == m, ids, big), axis=0, keepdims=True)
        s = jnp.where(ids == sel, NEG_INF, s)
        vals.append(m)
        sel_ids.append(sel)
    return jnp.concatenate(vals, axis=0), jnp.concatenate(sel_ids, axis=0)


def _gather_rows16(table, k):
    rows = lax.broadcasted_iota(jnp.int32, (PEER_TOPK, 1), 0).astype(F32)
    return jnp.sum(jnp.where(rows == k, table, 0.0), axis=0, keepdims=True)


def _peer_route_kernel(x_ref, mod_ref, g_ref, wq_ref, keys_ref, i1_ref, i2_ref, gate_ref):
    x = x_ref[...]
    mod = mod_ref[...]
    hn = _norm_mod(x, g_ref[...], mod[4:5], mod[3:4]).astype(BF16)
    q = jnp.dot(hn, wq_ref[...], preferred_element_type=F32).astype(BF16)

    key_ids = lax.broadcasted_iota(jnp.int32, (N_KEYS, 1), 0).astype(F32)
    n_cand = PEER_TOPK * SUBLANES + SUBLANES
    r = lax.broadcasted_iota(jnp.int32, (n_cand, 1), 0)
    flat = jnp.where(r < PEER_TOPK * SUBLANES, r + ((r >> 3) << 3),
                     r - (PEER_TOPK * SUBLANES - SUBLANES)).astype(F32)

    i1_all, i2_all, gate_all = [], [], []
    for h in range(PEER_HEADS):
        tops = []
        for p in range(2):
            hp = 2 * h + p
            st = lax.dot_general(keys_ref[hp], q[:, hp * PEER_HALF:(hp + 1) * PEER_HALF],
                                 (((1,), (1,)), ((), ())), preferred_element_type=F32)
            tops.append(_top16_rows(st, key_ids))
        (s1, k1), (s2, k2) = tops
        cands = [s1[a:a + 1] + s2[0:SUBLANES] for a in range(PEER_TOPK)]
        cands.append(s1[0:1] + s2[SUBLANES:])
        sc, pos = _top16_rows(jnp.concatenate(cands, axis=0), flat)
        pos1 = jnp.floor(pos * (1.0 / PEER_TOPK))
        pos2 = pos - pos1 * PEER_TOPK
        e1 = jnp.concatenate([_gather_rows16(k1, pos1[a:a + 1]) for a in range(PEER_TOPK)], axis=0)
        e2 = jnp.concatenate([_gather_rows16(k2, pos2[a:a + 1]) for a in range(PEER_TOPK)], axis=0)
        ex = jnp.exp(sc - sc[0:1])
        gate = ex / jnp.sum(ex, axis=0, keepdims=True)
        i1_all.append(e1)
        i2_all.append(e2)
        gate_all.append(gate)
    i1_ref[...] = jnp.concatenate(i1_all, axis=0).T
    i2_ref[...] = jnp.concatenate(i2_all, axis=0).T
    gate_ref[...] = jnp.concatenate(gate_all, axis=0).T


def _peer_route(x, mod4, layer, g, wq, keys, *, tt, n_ctx_tiles, tiles_per_lat_seq):
    t, d = x.shape
    nsel = PEER_HEADS * PEER_TOPK
    out = jax.ShapeDtypeStruct((t, nsel), F32)
    return pl.pallas_call(
        _peer_route_kernel,
        grid=(t // tt,),
        in_specs=[
            pl.BlockSpec((tt, d), lambda i: (i, 0)),
            pl.BlockSpec((None, None, 6, d), _mod_row_map(layer, n_ctx_tiles, tiles_per_lat_seq)),
            pl.BlockSpec((1, d), lambda i: (0, 0)),
            pl.BlockSpec(wq.shape, lambda i: (0, 0)),
            pl.BlockSpec(keys.shape, lambda i: (0, 0, 0)),
        ],
        out_specs=[pl.BlockSpec((tt, nsel), lambda i: (i, 0))] * 3,
        out_shape=[out, out, out],
        compiler_params=_cparams(("arbitrary",)),
        name="peer_route",
    )(x, mod4, g, wq, keys)


def _gate_pitch(tt):
    return tt + SUBLANES


def _peer_dense_kernel(final_norm, tt, keys_per_blk, x_ref, mod_ref, g_ref, i1_ref, i2_ref,
                       gate_ref, u_ref, v_ref, gf_ref, o_ref, hn_s, w_s, acc_s):
    eb = pl.program_id(1)
    pitch = _gate_pitch(tt)

    @pl.when(eb == 0)
    def _():
        mod = mod_ref[...]
        hn_s[...] = _norm_mod(x_ref[...], g_ref[...], mod[4:5], mod[3:4]).astype(BF16)
        acc_s[...] = jnp.zeros_like(acc_s)
        key_ids = lax.broadcasted_iota(jnp.int32, (N_KEYS, N_KEYS), 0).astype(F32)

        def build(t, carry):
            i1 = i1_ref[pl.ds(t, 1), :]
            i2 = i2_ref[pl.ds(t, 1), :]
            gt = gate_ref[pl.ds(t, 1), :]
            p = jnp.where(i1 == key_ids, gt, 0.0).astype(BF16)
            q = jnp.where(i2 == key_ids, 1.0, 0.0).astype(BF16)
            w = lax.dot_general(p, q, (((1,), (1,)), ((), ())), preferred_element_type=F32)
            w_s[pl.ds(t, N_KEYS, stride=pitch), :] = w
            return carry

        lax.fori_loop(0, tt, build, 0, unroll=32)

    act = lax.dot_general(hn_s[...], u_ref[...], (((1,), (1,)), ((), ())),
                          preferred_element_type=F32)
    hs = []
    for a in range(keys_per_blk):
        start = pl.multiple_of((eb * keys_per_blk + a) * pitch, SUBLANES)
        wa = w_s[pl.ds(start, tt), :]
        xa = act[:, a * N_KEYS:(a + 1) * N_KEYS]
        gelu = 0.5 * xa * (1.0 + lax.erf(xa * math.sqrt(0.5)))
        hs.append((gelu * wa).astype(BF16))
    h = jnp.concatenate(hs, axis=1)
    acc_s[...] += jnp.dot(h, v_ref[...], preferred_element_type=F32)

    @pl.when(eb == pl.num_programs(1) - 1)
    def _():
        y = x_ref[...] + mod_ref[...][5:6] * acc_s[...]
        if final_norm:
            ms = jnp.mean(y * y, axis=-1, keepdims=True)
            y = y * lax.rsqrt(ms + EPS) * gf_ref[...]
        o_ref[...] = y


def _peer_dense(x, mod4, layer, g, i1, i2, gate, u, v, g_final, *, tt, keys_per_blk,
                n_ctx_tiles, tiles_per_lat_seq, final_norm):
    t, d = x.shape
    n_exp = u.shape[0]
    eblk = keys_per_blk * N_KEYS
    nsel = i1.shape[1]
    pitch = _gate_pitch(tt)
    mod_map = _mod_row_map(layer, n_ctx_tiles, tiles_per_lat_seq)
    return pl.pallas_call(
        functools.partial(_peer_dense_kernel, final_norm, tt, keys_per_blk),
        grid=(t // tt, n_exp // eblk),
        in_specs=[
            pl.BlockSpec((tt, d), lambda i, e: (i, 0), pipeline_mode=pl.Buffered(1)),
            pl.BlockSpec((None, None, 6, d), mod_map),
            pl.BlockSpec((1, d), lambda i, e: (0, 0)),
            pl.BlockSpec((tt, nsel), lambda i, e: (i, 0), pipeline_mode=pl.Buffered(1)),
            pl.BlockSpec((tt, nsel), lambda i, e: (i, 0), pipeline_mode=pl.Buffered(1)),
            pl.BlockSpec((tt, nsel), lambda i, e: (i, 0), pipeline_mode=pl.Buffered(1)),
            pl.BlockSpec((eblk, d), lambda i, e: (e, 0)),
            pl.BlockSpec((eblk, d), lambda i, e: (e, 0)),
            pl.BlockSpec((1, d), lambda i, e: (0, 0)),
        ],
        out_specs=pl.BlockSpec((tt, d), lambda i, e: (i, 0)),
        out_shape=jax.ShapeDtypeStruct((t, d), F32),
        scratch_shapes=[
            pltpu.VMEM((tt, d), BF16),
            pltpu.VMEM((N_KEYS * pitch, N_KEYS), F32),
            pltpu.VMEM((tt, d), F32),
        ],
        compiler_params=_cparams(("arbitrary", "arbitrary")),
        name="peer_dense",
    )(x, mod4, g, i1, i2, gate, u, v, g_final)


def _softplus(x):
    return jnp.maximum(x, 0.0) + jnp.log(1.0 + jnp.exp(-jnp.abs(x)))


def _ssd_in_kernel(n_ctx_tiles, ctx_rowlen, x_ref, mod_ref, g_ref, wz_ref, wx_ref, wdt_ref,
                   cw_ref, cb_ref, dtb_ref, z_ref, xbc_ref, dt_ref):
    i = pl.program_id(0)
    x = x_ref[...]
    mod = mod_ref[...]
    hn = _norm_mod(x, g_ref[...], mod[1:2], mod[0:1]).astype(BF16)
    z_ref[...] = jnp.dot(hn, wz_ref[...], preferred_element_type=F32)
    xbc = jnp.dot(hn, wx_ref[...], preferred_element_type=F32)
    rowlen = jnp.where(i < n_ctx_tiles, ctx_rowlen, GRID_W)
    pos = lax.broadcasted_iota(jnp.int32, (x.shape[0], 1), 0) & (rowlen - 1)
    up, dn = _shift_rows(xbc, pos, rowlen)
    cw = cw_ref[...]
    xbc_ref[...] = _silu(up * cw[0:1] + xbc * cw[1:2] + dn * cw[2:3] + cb_ref[...])
    dt = jnp.dot(hn, wdt_ref[...], preferred_element_type=F32)
    dtb = dtb_ref[...]
    dt_ref[0] = _softplus(dt[:, :LANES] + dtb[0:1])
    dt_ref[1] = _softplus(dt[:, LANES:] + dtb[1:2])


def _ssd_in(x, mod4, layer, g, w_z, w_x, w_dt, conv_w, conv_b, dt_bias, *, tt, n_ctx_tiles,
            ctx_rowlen, tiles_per_lat_seq):
    t, d = x.shape
    dz, dx = w_z.shape[1], w_x.shape[1]
    const = lambda i: (0, 0)
    return pl.pallas_call(
        functools.partial(_ssd_in_kernel, n_ctx_tiles, ctx_rowlen),
        grid=(t // tt,),
        in_specs=[
            pl.BlockSpec((tt, d), lambda i: (i, 0)),
            pl.BlockSpec((None, None, 6, d), _mod_row_map(layer, n_ctx_tiles, tiles_per_lat_seq)),
            pl.BlockSpec((1, d), const),
            pl.BlockSpec((d, dz), const),
            pl.BlockSpec((d, dx), const),
            pl.BlockSpec((d, 2 * LANES), const),
            pl.BlockSpec((CONV_W, dx), const),
            pl.BlockSpec((1, dx), const),
            pl.BlockSpec((2, LANES), const),
        ],
        out_specs=[
            pl.BlockSpec((tt, dz), lambda i: (i, 0)),
            pl.BlockSpec((tt, dx), lambda i: (i, 0)),
            pl.BlockSpec((2, tt, LANES), lambda i: (0, i, 0)),
        ],
        out_shape=[
            jax.ShapeDtypeStruct((t, dz), F32),
            jax.ShapeDtypeStruct((t, dx), F32),
            jax.ShapeDtypeStruct((2, t, LANES), F32),
        ],
        compiler_params=_cparams(("arbitrary",)),
        name="ssd_in",
    )(x, mod4, g, w_z, w_x, w_dt, conv_w, conv_b, dt_bias)


def _scan_chunk(direction, step, n_chunks):
    return jnp.where(direction == 0, step, n_chunks - 1 - step)


def _ssd_scan_kernel(n_heads, ctx_chunks, ctx_nc, lat_nc, xbc_ref, dt_ref, a_ref, h0_ref,
                     y_ref, st_ref, state_s):
    direction = pl.program_id(0)
    chunk = _scan_chunk(direction, pl.program_id(1), pl.num_programs(1))
    q = SSD_CHUNK
    p = SSD_HEAD_DIM
    d_inner = n_heads * p
    gn = SSD_GROUPS * SSD_STATE
    heads_per_group = n_heads // SSD_GROUPS

    is_ctx = chunk < ctx_chunks
    seq_nc = jnp.where(is_ctx, ctx_nc, lat_nc)
    local = jnp.where(is_ctx, lax.rem(chunk, ctx_nc), lax.rem(chunk - ctx_chunks, lat_nc))
    seq_first = local == jnp.where(direction == 0, 0, seq_nc - 1)
    seq_last = local == jnp.where(direction == 0, seq_nc - 1, 0)

    @pl.when(jnp.logical_and(seq_first, is_ctx))
    def _():
        state_s[...] = jnp.zeros_like(state_s)

    @pl.when(jnp.logical_and(seq_first, jnp.logical_not(is_ctx)))
    def _():
        state_s[...] = h0_ref[...]

    row = lax.broadcasted_iota(jnp.int32, (q, q), 0)
    col = lax.broadcasted_iota(jnp.int32, (q, q), 1)
    fwd = direction == 0
    mask = (col - row) * jnp.where(fwd, 1, -1) <= 0
    tri = jnp.where(mask, 1.0, 0.0)

    dt = dt_ref[...]
    dta = dt * (-jnp.exp(a_ref[...]))
    cum = jnp.dot(tri, dta, preferred_element_type=F32, precision=lax.Precision.HIGHEST)
    cum_t = cum.T
    tot = jnp.where(fwd, cum[q - 1:q, :], cum[0:1, :])
    exp_cum = jnp.exp(cum)
    dt_end = dt * jnp.exp(tot - cum)
    exp_tot = jnp.exp(tot)

    lane = lax.broadcasted_iota(jnp.int32, (q, 2 * p), 1)
    sub = lax.broadcasted_iota(jnp.int32, (2 * p, 1), 0)
    first = lane < p

    def pair_cols(mat, hd):
        return jnp.where(first, mat[:, hd:hd + 1], mat[:, hd + 1:hd + 2])

    for g in range(SSD_GROUPS):
        bm = xbc_ref[:, d_inner + g * SSD_STATE:d_inner + (g + 1) * SSD_STATE].astype(BF16)
        cm = xbc_ref[:, d_inner + gn + g * SSD_STATE:d_inner + gn + (g + 1) * SSD_STATE].astype(BF16)
        cb = lax.dot_general(cm, bm, (((1,), (1,)), ((), ())), preferred_element_type=F32)
        for pr in range(heads_per_group // 2):
            hd = g * heads_per_group + 2 * pr
            pair = hd // 2
            xs = xbc_ref[:, hd * p:(hd + 2) * p]
            xdt = xs * pair_cols(dt, hd)
            y = None
            for k in range(2):
                seg = cum[:, hd + k:hd + k + 1] - cum_t[hd + k:hd + k + 1, :]
                m = (jnp.where(mask, jnp.exp(seg), 0.0) * cb).astype(BF16)
                rhs = jnp.where(first if k == 0 else jnp.logical_not(first), xdt, 0.0).astype(BF16)
                yk = jnp.dot(m, rhs, preferred_element_type=F32)
                y = yk if y is None else y + yk
            h_prev = state_s[pair]
            y_off = lax.dot_general(cm, h_prev.astype(BF16), (((1,), (1,)), ((), ())),
                                    preferred_element_type=F32)
            y_ref[:, hd * p:(hd + 2) * p] = y + y_off * pair_cols(exp_cum, hd)
            xdtw = (xs * pair_cols(dt_end, hd)).astype(BF16)
            upd = lax.dot_general(xdtw, bm, (((0,), (0,)), ((), ())), preferred_element_type=F32)
            dec = jnp.where(sub < p, exp_tot[:, hd:hd + 1], exp_tot[:, hd + 1:hd + 2])
            state_s[pair] = dec * h_prev + upd

    @pl.when(jnp.logical_and(seq_last, is_ctx))
    def _():
        st_ref[...] = state_s[...]


def _ssd_scan(xbc, dt, a_log, h0, *, n_heads, n_ctx_seq, ctx_nc, lat_nc):
    t, dx = xbc.shape
    d_inner = n_heads * SSD_HEAD_DIM
    n_pairs = n_heads // 2
    q = SSD_CHUNK
    n_chunks = t // q
    ctx_chunks = n_ctx_seq * ctx_nc
    state_block = (None, None, n_pairs, 2 * SSD_HEAD_DIM, SSD_STATE)

    def chunk_of(d, c):
        return _scan_chunk(d, c, n_chunks)

    def h0_map(d, c):
        return (jnp.maximum(chunk_of(d, c) - ctx_chunks, 0) // lat_nc, d, 0, 0, 0)

    def st_map(d, c):
        return (jnp.minimum(chunk_of(d, c), ctx_chunks - 1) // ctx_nc, d, 0, 0, 0)

    return pl.pallas_call(
        functools.partial(_ssd_scan_kernel, n_heads, ctx_chunks, ctx_nc, lat_nc),
        grid=(2, n_chunks),
        in_specs=[
            pl.BlockSpec((q, dx), lambda d, c: (chunk_of(d, c), 0)),
            pl.BlockSpec((None, q, LANES), lambda d, c: (d, chunk_of(d, c), 0)),
            pl.BlockSpec((None, 1, LANES), lambda d, c: (d, 0, 0)),
            pl.BlockSpec(state_block, h0_map),
        ],
        out_specs=[
            pl.BlockSpec((None, q, d_inner), lambda d, c: (d, chunk_of(d, c), 0)),
            pl.BlockSpec(state_block, st_map),
        ],
        out_shape=[
            jax.ShapeDtypeStruct((2, t, d_inner), F32),
            jax.ShapeDtypeStruct((n_ctx_seq, 2, n_pairs, 2 * SSD_HEAD_DIM, SSD_STATE), F32),
        ],
        scratch_shapes=[pltpu.VMEM((n_pairs, 2 * SSD_HEAD_DIM, SSD_STATE), F32)],
        compiler_params=_cparams(("arbitrary", "arbitrary")),
        name="ssd_scan",
    )(xbc, dt, a_log, h0)


def _ssd_out_kernel(x_ref, mod_ref, y_ref, xs_ref, z_ref, dskip_ref, ng_ref, wout_ref, o_ref):
    y = y_ref[0] + y_ref[1] + dskip_ref[...] * xs_ref[...]
    yz = y * _silu(z_ref[...])
    ms = jnp.mean(yz * yz, axis=-1, keepdims=True)
    yn = (yz * lax.rsqrt(ms + EPS) * ng_ref[...]).astype(BF16)
    mix = jnp.dot(yn, wout_ref[...], preferred_element_type=F32)
    o_ref[...] = x_ref[...] + mod_ref[...][2:3] * mix


def _ssd_out(x, mod4, layer, y, xbc, z, d_skip, norm_g, w_out, *, tt, n_ctx_tiles,
             tiles_per_lat_seq):
    t, d = x.shape
    di = z.shape[1]
    const = lambda i: (0, 0)
    return pl.pallas_call(
        _ssd_out_kernel,
        grid=(t // tt,),
        in_specs=[
            pl.BlockSpec((tt, d), lambda i: (i, 0)),
            pl.BlockSpec((None, None, 6, d), _mod_row_map(layer, n_ctx_tiles, tiles_per_lat_seq)),
            pl.BlockSpec((2, tt, di), lambda i: (0, i, 0)),
            pl.BlockSpec((tt, di), lambda i: (i, 0)),
            pl.BlockSpec((tt, di), lambda i: (i, 0)),
            pl.BlockSpec((1, di), const),
            pl.BlockSpec((1, di), const),
            pl.BlockSpec((di, d), const),
        ],
        out_specs=pl.BlockSpec((tt, d), lambda i: (i, 0)),
        out_shape=jax.ShapeDtypeStruct((t, d), F32),
        compiler_params=_cparams(("arbitrary",)),
        name="ssd_out",
    )(x, mod4, y, xbc, z, d_skip, norm_g, w_out)


def _pick_tile(candidates, *lengths):
    for tt in candidates:
        if all(n % tt == 0 for n in lengths):
            return tt
    raise ValueError(f"no tile in {candidates} divides {lengths}")


def kernel(x_prompt, x_sample, state_ssm, c, c_ctx, norm_mix_g, norm_ffn_g, norm_f_g, ada_w, ada_b, sc_w_in, sc_conv_w, sc_w_out, ssd_w_in, ssd_conv_w, ssd_conv_b, ssd_dt_bias, ssd_a_log, ssd_d, ssd_norm_g, ssd_w_out, peer_wq, peer_keys, peer_u, peer_v):
    n_ctx_seq, ctx_len, d = x_prompt.shape
    n_lat_seq, lat_len, _ = x_sample.shape
    depth = ada_w.shape[0]
    t_ctx = n_ctx_seq * ctx_len
    d_inner = ssd_norm_g.shape[1]
    n_heads = d_inner // SSD_HEAD_DIM
    conv_dim = ssd_conv_w.shape[2]
    assert ctx_len & (ctx_len - 1) == 0 and ctx_len % SSD_CHUNK == 0 and lat_len % SSD_CHUNK == 0
    assert n_heads % (2 * SSD_GROUPS) == 0 and n_heads <= LANES

    x = jnp.concatenate([x_prompt.reshape(t_ctx, d), x_sample.reshape(n_lat_seq * lat_len, d)], axis=0)
    n_cond = 1 + n_lat_seq
    cond_rows = -(-n_cond // SUBLANES) * SUBLANES
    cond = jnp.concatenate([c_ctx[None], c, jnp.zeros((cond_rows - n_cond, d), F32)], axis=0)
    mod4 = _adaln(cond, ada_w, ada_b).reshape(depth, cond_rows, 6, d)

    def tiling(tt):
        return dict(tt=tt, n_ctx_tiles=t_ctx // tt, tiles_per_lat_seq=lat_len // tt)

    tt_mix = _pick_tile((512, 256), t_ctx, lat_len)
    tt_route = _pick_tile((256, 128), t_ctx, lat_len)
    tt_dense = _pick_tile((512, 256), t_ctx, lat_len)
    assert tt_mix % ctx_len == 0 and tt_mix % GRID_W == 0

    states = []
    for i in range(depth):
        j = i // 2
        if i % 2 == 0:
            x = _conv_mixer(x, mod4, i, norm_mix_g[i][None], sc_w_in[j].astype(BF16), sc_conv_w[j],
                            sc_w_out[j].astype(BF16), ctx_rowlen=ctx_len, **tiling(tt_mix))
        else:
            w_in = ssd_w_in[j]
            w_z = w_in[:, :d_inner].astype(BF16)
            w_x = w_in[:, d_inner:d_inner + conv_dim].astype(BF16)
            w_dt_raw = w_in[:, d_inner + conv_dim:]
            pad = jnp.zeros((d, LANES - n_heads), F32)
            w_dt = jnp.concatenate([w_dt_raw[:, :n_heads], pad, w_dt_raw[:, n_heads:], pad], axis=1).astype(BF16)
            lane_pad = ((0, 0), (0, LANES - n_heads))
            dt_bias = jnp.pad(ssd_dt_bias[j], lane_pad)
            a_log = jnp.pad(ssd_a_log[j], lane_pad)[:, None, :]
            z, xbc, dt = _ssd_in(x, mod4, i, norm_mix_g[i][None], w_z, w_x, w_dt, ssd_conv_w[j],
                                 ssd_conv_b[j][None], dt_bias, ctx_rowlen=ctx_len, **tiling(tt_mix))
            n_pairs = n_heads // 2
            h0 = state_ssm[:, j].reshape(n_lat_seq, 2, n_pairs, 2 * SSD_HEAD_DIM, SSD_STATE)
            y, st = _ssd_scan(xbc, dt, a_log, h0, n_heads=n_heads, n_ctx_seq=n_ctx_seq,
                              ctx_nc=ctx_len // SSD_CHUNK, lat_nc=lat_len // SSD_CHUNK)
            states.append(st.reshape(n_ctx_seq, 2, n_heads, SSD_HEAD_DIM, SSD_STATE))
            d_skip = jnp.repeat(ssd_d[j][0] + ssd_d[j][1], SSD_HEAD_DIM)[None]
            x = _ssd_out(x, mod4, i, y, xbc, z, d_skip, ssd_norm_g[j][None],
                         ssd_w_out[j].astype(BF16), **tiling(tt_mix))
        keys = peer_keys[i].reshape(2 * PEER_HEADS, N_KEYS, PEER_HALF).astype(BF16)
        i1, i2, gate = _peer_route(x, mod4, i, norm_ffn_g[i][None], peer_wq[i].astype(BF16), keys,
                                   **tiling(tt_route))
        x = _peer_dense(x, mod4, i, norm_ffn_g[i][None], i1, i2, gate, peer_u[i].astype(BF16),
                        peer_v[i].astype(BF16), norm_f_g[None], keys_per_blk=8,
                        final_norm=(i == depth - 1), **tiling(tt_dense))

    y_prompt = x[:t_ctx].reshape(n_ctx_seq, ctx_len, d)
    y_sample = x[t_ctx:].reshape(n_lat_seq, lat_len, d)
    return y_prompt, y_sample, jnp.stack(states, axis=1)
```

```python
import functools
import math

import jax
import jax.numpy as jnp
from jax import lax
from jax.experimental import pallas as pl
from jax.experimental.pallas import tpu as pltpu

F32 = jnp.float32
BF16 = jnp.bfloat16
EPS = 1e-6

LANES = 128
SUBLANES = 8
VMEM_LIMIT_BYTES = 56 * 1024 * 1024

GRID_W = 64
CONV_W = 3
SSD_HEAD_DIM = 64
SSD_STATE = 128
SSD_GROUPS = 8
SSD_CHUNK = 128
N_KEYS = 128
PEER_HEADS = 8
PEER_TOPK = 16
PEER_HALF = 128

NEG_INF = float("-inf")


def _cparams(sem):
    return pltpu.CompilerParams(dimension_semantics=sem, vmem_limit_bytes=VMEM_LIMIT_BYTES)


def _norm_mod(x, g, scl, sh):
    ms = jnp.mean(x * x, axis=-1, keepdims=True)
    return x * lax.rsqrt(ms + EPS) * g * (1.0 + scl) + sh


def _silu(x):
    return x * (1.0 / (1.0 + jnp.exp(-x)))


def _mod_row_map(layer, n_ctx_tiles, tiles_per_lat_seq):
    def index_map(i, *_):
        row = jnp.where(i < n_ctx_tiles, 0, 1 + (i - n_ctx_tiles) // tiles_per_lat_seq)
        return (layer, row, 0, 0)
    return index_map


def _shift_rows(u, pos, rowlen):
    n = u.shape[0]
    up = pltpu.roll(u, 1, axis=0)
    dn = pltpu.roll(u, n - 1, axis=0)
    up = jnp.where(pos == 0, 0.0, up)
    dn = jnp.where(pos == rowlen - 1, 0.0, dn)
    return up, dn


def _adaln_kernel(c_ref, w_ref, b_ref, o_ref):
    s = _silu(c_ref[...])
    o_ref[...] = jnp.dot(s, w_ref[...], preferred_element_type=F32,
                         precision=lax.Precision.HIGHEST) + b_ref[...]


def _adaln(cond, ada_w, ada_b):
    depth, d, n = ada_w.shape
    rows = cond.shape[0]
    tn = 1536
    return pl.pallas_call(
        _adaln_kernel,
        grid=(depth, n // tn),
        in_specs=[
            pl.BlockSpec((rows, d), lambda l, j: (0, 0)),
            pl.BlockSpec((None, d, tn), lambda l, j: (l, 0, j)),
            pl.BlockSpec((None, 1, tn), lambda l, j: (l, 0, j)),
        ],
        out_specs=pl.BlockSpec((None, rows, tn), lambda l, j: (l, 0, j)),
        out_shape=jax.ShapeDtypeStruct((depth, rows, n), F32),
        compiler_params=_cparams(("arbitrary", "arbitrary")),
        name="adaln",
    )(cond, ada_w, ada_b.reshape(depth, 1, n))


def _conv_mixer_kernel(n_ctx_tiles, ctx_rowlen, x_ref, mod_ref, g_ref, win_ref, cw_ref,
                       wout_ref, o_ref):
    i = pl.program_id(0)
    x = x_ref[...]
    d = x.shape[1]
    mod = mod_ref[...]
    hn = _norm_mod(x, g_ref[...], mod[1:2], mod[0:1]).astype(BF16)
    p = jnp.dot(hn, win_ref[...], preferred_element_type=F32)
    bg, cg, xv = p[:, :d], p[:, d:2 * d], p[:, 2 * d:]
    u = cg * xv
    rowlen = jnp.where(i < n_ctx_tiles, ctx_rowlen, GRID_W)
    pos = lax.broadcasted_iota(jnp.int32, (x.shape[0], 1), 0) & (rowlen - 1)
    up, dn = _shift_rows(u, pos, rowlen)
    cw = cw_ref[...]
    y = up * cw[0:1] + u * cw[1:2] + dn * cw[2:3]
    mix = jnp.dot((bg * y).astype(BF16), wout_ref[...], preferred_element_type=F32)
    o_ref[...] = x + mod[2:3] * mix


def _conv_mixer(x, mod4, layer, g, w_in, conv_w, w_out, *, tt, n_ctx_tiles, ctx_rowlen,
                tiles_per_lat_seq):
    t, d = x.shape
    return pl.pallas_call(
        functools.partial(_conv_mixer_kernel, n_ctx_tiles, ctx_rowlen),
        grid=(t // tt,),
        in_specs=[
            pl.BlockSpec((tt, d), lambda i: (i, 0)),
            pl.BlockSpec((None, None, 6, d), _mod_row_map(layer, n_ctx_tiles, tiles_per_lat_seq)),
            pl.BlockSpec((1, d), lambda i: (0, 0)),
            pl.BlockSpec((d, 3 * d), lambda i: (0, 0)),
            pl.BlockSpec((CONV_W, d), lambda i: (0, 0)),
            pl.BlockSpec((d, d), lambda i: (0, 0)),
        ],
        out_specs=pl.BlockSpec((tt, d), lambda i: (i, 0)),
        out_shape=jax.ShapeDtypeStruct((t, d), F32),
        compiler_params=_cparams(("arbitrary",)),
        name="conv_mixer",
    )(x, mod4, g, w_in, conv_w, w_out)


ROUTE_TILE = SUBLANES * LANES


def _oddeven_merge_sort_pairs(n):
    pairs = []
    p = 1
    while p < n:
        k = p
        while k >= 1:
            for j in range(k % p, n - k, 2 * k):
                for i in range(min(k, n - j - k)):
                    if (i + j) // (2 * p) == (i + j + k) // (2 * p):
                        pairs.append((i + j, i + j + k))
            k //= 2
        p *= 2
    return pairs


_SORT16 = _oddeven_merge_sort_pairs(PEER_TOPK)


def _goes_first(a, ia, b, ib):
    return (a > b) | ((a == b) & (ia < ib))


def _compare_exchange(v, ix, i, j, ids_ordered=False):
    a, b, ia, ib = v[i], v[j], ix[i], ix[j]
    f = (a >= b) if ids_ordered else _goes_first(a, ia, b, ib)
    v[i], v[j] = jnp.where(f, a, b), jnp.where(f, b, a)
    ix[i], ix[j] = jnp.where(f, ia, ib), jnp.where(f, ib, ia)


def _sort16(v, ix):
    for n, (i, j) in enumerate(_SORT16):
        _compare_exchange(v, ix, i, j, ids_ordered=n < PEER_TOPK // 2)


def _bitonic_sort16(v, ix):
    d = PEER_TOPK // 2
    while d >= 1:
        for i in range(PEER_TOPK):
            if i & d == 0:
                _compare_exchange(v, ix, i, i + d)
        d //= 2


def _merge_top16(rv, ri, xv, xi, sort=True):
    for k in range(len(xv)):
        i = PEER_TOPK - 1 - k
        f = _goes_first(rv[i], ri[i], xv[k], xi[k])
        rv[i] = jnp.where(f, rv[i], xv[k])
        ri[i] = jnp.where(f, ri[i], xi[k])
    if sort:
        _bitonic_sort16(rv, ri)


def _vreg_rows(k):
    return pl.ds(k * SUBLANES, SUBLANES)


def _peer_route_kernel(x_ref, mod_ref, g_ref, wq_ref, keys_ref, i1_ref, i2_ref, gate_ref,
                       hn_s, q_s, sc_s, topv_s, topi_s, e1_s, e2_s, gt_s):
    mod = mod_ref[...]
    hn_s[...] = _norm_mod(x_ref[...], g_ref[...], mod[4:5], mod[3:4]).astype(BF16)
    for h in range(PEER_HEADS):
        qh = jnp.dot(hn_s[...], wq_ref[:, 2 * h * PEER_HALF:2 * (h + 1) * PEER_HALF],
                     preferred_element_type=F32).astype(BF16)
        q_s[2 * h] = qh[:, :PEER_HALF]
        q_s[2 * h + 1] = qh[:, PEER_HALF:]

    def stage1(hp, carry):
        st = lax.dot_general(keys_ref[hp], q_s[hp], (((1,), (1,)), ((), ())),
                             preferred_element_type=F32)
        for c in range(SUBLANES):
            sc_s[pl.ds(c, N_KEYS, stride=SUBLANES), :] = st[:, c * LANES:(c + 1) * LANES]
        rv = ri = None
        for grp in range(N_KEYS // PEER_TOPK):
            v = [sc_s[_vreg_rows(grp * PEER_TOPK + i), :] for i in range(PEER_TOPK)]
            ix = [float(grp * PEER_TOPK + i) for i in range(PEER_TOPK)]
            _sort16(v, ix)
            if rv is None:
                rv, ri = v, ix
            else:
                _merge_top16(rv, ri, v, ix)
        for k in range(PEER_TOPK):
            topv_s[hp, _vreg_rows(k), :] = rv[k]
            topi_s[hp, _vreg_rows(k), :] = ri[k]
        return carry

    lax.fori_loop(0, 2 * PEER_HEADS, stage1, 0)

    n_wide = PEER_TOPK // 2
    cand_lists = [[(a, b) for b in range(PEER_TOPK // (a + 1))] for a in range(n_wide)]
    cand_lists.append([(a, 0) for a in range(n_wide, PEER_TOPK)])

    def stage2(h, carry):
        s1 = [topv_s[2 * h, _vreg_rows(k), :] for k in range(PEER_TOPK)]
        s2 = [topv_s[2 * h + 1, _vreg_rows(k), :] for k in range(PEER_TOPK)]
        rv = rf = None
        for n, pairs in enumerate(cand_lists):
            v = [s1[a] + s2[b] for a, b in pairs]
            flat = [float(a * PEER_TOPK + b) for a, b in pairs]
            if rv is None:
                rv, rf = v, flat
            else:
                _merge_top16(rv, rf, v, flat, sort=n < len(cand_lists) - 1)
        top = s1[0] + s2[0]
        ex = [jnp.exp(v - top) for v in rv]
        denom = ex[0]
        for e in ex[1:]:
            denom = denom + e
        inv = 1.0 / denom
        k1 = [topi_s[2 * h, _vreg_rows(k), :] for k in range(PEER_TOPK)]
        k2 = [topi_s[2 * h + 1, _vreg_rows(k), :] for k in range(PEER_TOPK)]
        for k in range(PEER_TOPK):
            pos1 = jnp.floor(rf[k] * (1.0 / PEER_TOPK))
            pos2 = rf[k] - pos1 * PEER_TOPK
            e1 = jnp.zeros_like(top)
            e2 = jnp.zeros_like(top)
            for a in range(PEER_TOPK):
                e1 = jnp.where(pos1 == a, k1[a], e1)
                e2 = jnp.where(pos2 == a, k2[a], e2)
            rows = pl.ds(pl.multiple_of((h * PEER_TOPK + k) * SUBLANES, SUBLANES), SUBLANES)
            e1_s[rows, :] = e1
            e2_s[rows, :] = e2
            gt_s[rows, :] = ex[k] * inv
        return carry

    lax.fori_loop(0, PEER_HEADS, stage2, 0)

    for src, dst in ((e1_s, i1_ref), (e2_s, i2_ref), (gt_s, gate_ref)):
        for c in range(SUBLANES):
            dst[c * LANES:(c + 1) * LANES, :] = src[pl.ds(c, LANES, stride=SUBLANES), :].T


def _peer_route(x, mod4, layer, g, wq, keys, *, tt, n_ctx_tiles, tiles_per_lat_seq):
    t, d = x.shape
    nsel = PEER_HEADS * PEER_TOPK
    assert tt == ROUTE_TILE and nsel == LANES
    out = jax.ShapeDtypeStruct((t, nsel), F32)
    vreg_table = pltpu.VMEM((nsel * SUBLANES, LANES), F32)
    return pl.pallas_call(
        _peer_route_kernel,
        grid=(t // tt,),
        scratch_shapes=[
            pltpu.VMEM((tt, d), BF16),
            pltpu.VMEM((2 * PEER_HEADS, tt, PEER_HALF), BF16),
            pltpu.VMEM((N_KEYS * SUBLANES, LANES), F32),
            pltpu.VMEM((2 * PEER_HEADS, PEER_TOPK * SUBLANES, LANES), F32),
            pltpu.VMEM((2 * PEER_HEADS, PEER_TOPK * SUBLANES, LANES), F32),
            vreg_table, vreg_table, vreg_table,
        ],
        in_specs=[
            pl.BlockSpec((tt, d), lambda i: (i, 0)),
            pl.BlockSpec((None, None, 6, d), _mod_row_map(layer, n_ctx_tiles, tiles_per_lat_seq)),
            pl.BlockSpec((1, d), lambda i: (0, 0)),
            pl.BlockSpec(wq.shape, lambda i: (0, 0)),
            pl.BlockSpec(keys.shape, lambda i: (0, 0, 0)),
        ],
        out_specs=[pl.BlockSpec((tt, nsel), lambda i: (i, 0))] * 3,
        out_shape=[out, out, out],
        compiler_params=_cparams(("arbitrary",)),
        name="peer_route",
    )(x, mod4, g, wq, keys)


def _gate_pitch(tt):
    return tt + SUBLANES


def _peer_dense_kernel(final_norm, tt, keys_per_blk, x_ref, mod_ref, g_ref, i1_ref, i2_ref,
                       gate_ref, u_ref, v_ref, gf_ref, o_ref, hn_s, w_s, acc_s):
    eb = pl.program_id(1)
    pitch = _gate_pitch(tt)

    @pl.when(eb == 0)
    def _():
        mod = mod_ref[...]
        hn_s[...] = _norm_mod(x_ref[...], g_ref[...], mod[4:5], mod[3:4]).astype(BF16)
        acc_s[...] = jnp.zeros_like(acc_s)
        key_ids = lax.broadcasted_iota(jnp.int32, (N_KEYS, N_KEYS), 0).astype(F32)

        def build(t, carry):
            i1 = i1_ref[pl.ds(t, 1), :]
            i2 = i2_ref[pl.ds(t, 1), :]
            gt = gate_ref[pl.ds(t, 1), :]
            p = jnp.where(i1 == key_ids, gt, 0.0).astype(BF16)
            q = jnp.where(i2 == key_ids, 1.0, 0.0).astype(BF16)
            w = lax.dot_general(p, q, (((1,), (1,)), ((), ())), preferred_element_type=F32)
            w_s[pl.ds(t, N_KEYS, stride=pitch), :] = w
            return carry

        lax.fori_loop(0, tt, build, 0, unroll=32)

    act = lax.dot_general(hn_s[...], u_ref[...], (((1,), (1,)), ((), ())),
                          preferred_element_type=F32)
    hs = []
    for a in range(keys_per_blk):
        start = pl.multiple_of((eb * keys_per_blk + a) * pitch, SUBLANES)
        wa = w_s[pl.ds(start, tt), :]
        xa = act[:, a * N_KEYS:(a + 1) * N_KEYS]
        gelu = 0.5 * xa * (1.0 + lax.erf(xa * math.sqrt(0.5)))
        hs.append((gelu * wa).astype(BF16))
    h = jnp.concatenate(hs, axis=1)
    acc_s[...] += jnp.dot(h, v_ref[...], preferred_element_type=F32)

    @pl.when(eb == pl.num_programs(1) - 1)
    def _():
        y = x_ref[...] + mod_ref[...][5:6] * acc_s[...]
        if final_norm:
            ms = jnp.mean(y * y, axis=-1, keepdims=True)
            y = y * lax.rsqrt(ms + EPS) * gf_ref[...]
        o_ref[...] = y


def _peer_dense(x, mod4, layer, g, i1, i2, gate, u, v, g_final, *, tt, keys_per_blk,
                n_ctx_tiles, tiles_per_lat_seq, final_norm):
    t, d = x.shape
    n_exp = u.shape[0]
    eblk = keys_per_blk * N_KEYS
    nsel = i1.shape[1]
    pitch = _gate_pitch(tt)
    mod_map = _mod_row_map(layer, n_ctx_tiles, tiles_per_lat_seq)
    return pl.pallas_call(
        functools.partial(_peer_dense_kernel, final_norm, tt, keys_per_blk),
        grid=(t // tt, n_exp // eblk),
        in_specs=[
            pl.BlockSpec((tt, d), lambda i, e: (i, 0), pipeline_mode=pl.Buffered(1)),
            pl.BlockSpec((None, None, 6, d), mod_map),
            pl.BlockSpec((1, d), lambda i, e: (0, 0)),
            pl.BlockSpec((tt, nsel), lambda i, e: (i, 0), pipeline_mode=pl.Buffered(1)),
            pl.BlockSpec((tt, nsel), lambda i, e: (i, 0), pipeline_mode=pl.Buffered(1)),
            pl.BlockSpec((tt, nsel), lambda i, e: (i, 0), pipeline_mode=pl.Buffered(1)),
            pl.BlockSpec((eblk, d), lambda i, e: (e, 0)),
            pl.BlockSpec((eblk, d), lambda i, e: (e, 0)),
            pl.BlockSpec((1, d), lambda i, e: (0, 0)),
        ],
        out_specs=pl.BlockSpec((tt, d), lambda i, e: (i, 0)),
        out_shape=jax.ShapeDtypeStruct((t, d), F32),
        scratch_shapes=[
            pltpu.VMEM((tt, d), BF16),
            pltpu.VMEM((N_KEYS * pitch, N_KEYS), F32),
            pltpu.VMEM((tt, d), F32),
        ],
        compiler_params=_cparams(("arbitrary", "arbitrary")),
        name="peer_dense",
    )(x, mod4, g, i1, i2, gate, u, v, g_final)


def _softplus(x):
    return jnp.maximum(x, 0.0) + jnp.log(1.0 + jnp.exp(-jnp.abs(x)))


def _ssd_in_kernel(n_ctx_tiles, ctx_rowlen, x_ref, mod_ref, g_ref, wz_ref, wx_ref, wdt_ref,
                   cw_ref, cb_ref, dtb_ref, z_ref, xbc_ref, dt_ref):
    i = pl.program_id(0)
    x = x_ref[...]
    mod = mod_ref[...]
    hn = _norm_mod(x, g_ref[...], mod[1:2], mod[0:1]).astype(BF16)
    z_ref[...] = jnp.dot(hn, wz_ref[...], preferred_element_type=F32)
    xbc = jnp.dot(hn, wx_ref[...], preferred_element_type=F32)
    rowlen = jnp.where(i < n_ctx_tiles, ctx_rowlen, GRID_W)
    pos = lax.broadcasted_iota(jnp.int32, (x.shape[0], 1), 0) & (rowlen - 1)
    up, dn = _shift_rows(xbc, pos, rowlen)
    cw = cw_ref[...]
    xbc_ref[...] = _silu(up * cw[0:1] + xbc * cw[1:2] + dn * cw[2:3] + cb_ref[...])
    dt = jnp.dot(hn, wdt_ref[...], preferred_element_type=F32)
    dtb = dtb_ref[...]
    dt_ref[0] = _softplus(dt[:, :LANES] + dtb[0:1])
    dt_ref[1] = _softplus(dt[:, LANES:] + dtb[1:2])


def _ssd_in(x, mod4, layer, g, w_z, w_x, w_dt, conv_w, conv_b, dt_bias, *, tt, n_ctx_tiles,
            ctx_rowlen, tiles_per_lat_seq):
    t, d = x.shape
    dz, dx = w_z.shape[1], w_x.shape[1]
    const = lambda i: (0, 0)
    return pl.pallas_call(
        functools.partial(_ssd_in_kernel, n_ctx_tiles, ctx_rowlen),
        grid=(t // tt,),
        in_specs=[
            pl.BlockSpec((tt, d), lambda i: (i, 0)),
            pl.BlockSpec((None, None, 6, d), _mod_row_map(layer, n_ctx_tiles, tiles_per_lat_seq)),
            pl.BlockSpec((1, d), const),
            pl.BlockSpec((d, dz), const),
            pl.BlockSpec((d, dx), const),
            pl.BlockSpec((d, 2 * LANES), const),
            pl.BlockSpec((CONV_W, dx), const),
            pl.BlockSpec((1, dx), const),
            pl.BlockSpec((2, LANES), const),
        ],
        out_specs=[
            pl.BlockSpec((tt, dz), lambda i: (i, 0)),
            pl.BlockSpec((tt, dx), lambda i: (i, 0)),
            pl.BlockSpec((2, tt, LANES), lambda i: (0, i, 0)),
        ],
        out_shape=[
            jax.ShapeDtypeStruct((t, dz), F32),
            jax.ShapeDtypeStruct((t, dx), F32),
            jax.ShapeDtypeStruct((2, t, LANES), F32),
        ],
        compiler_params=_cparams(("arbitrary",)),
        name="ssd_in",
    )(x, mod4, g, w_z, w_x, w_dt, conv_w, conv_b, dt_bias)


def _scan_chunk(direction, step, n_chunks):
    return jnp.where(direction == 0, step, n_chunks - 1 - step)


def _ssd_scan_kernel(n_heads, ctx_chunks, ctx_nc, lat_nc, xbc_ref, dt_ref, a_ref, h0_ref,
                     y_ref, st_ref, state_s):
    direction = pl.program_id(0)
    chunk = _scan_chunk(direction, pl.program_id(1), pl.num_programs(1))
    q = SSD_CHUNK
    p = SSD_HEAD_DIM
    d_inner = n_heads * p
    gn = SSD_GROUPS * SSD_STATE
    heads_per_group = n_heads // SSD_GROUPS

    is_ctx = chunk < ctx_chunks
    seq_nc = jnp.where(is_ctx, ctx_nc, lat_nc)
    local = jnp.where(is_ctx, lax.rem(chunk, ctx_nc), lax.rem(chunk - ctx_chunks, lat_nc))
    seq_first = local == jnp.where(direction == 0, 0, seq_nc - 1)
    seq_last = local == jnp.where(direction == 0, seq_nc - 1, 0)

    @pl.when(jnp.logical_and(seq_first, is_ctx))
    def _():
        state_s[...] = jnp.zeros_like(state_s)

    @pl.when(jnp.logical_and(seq_first, jnp.logical_not(is_ctx)))
    def _():
        state_s[...] = h0_ref[...]

    row = lax.broadcasted_iota(jnp.int32, (q, q), 0)
    col = lax.broadcasted_iota(jnp.int32, (q, q), 1)
    fwd = direction == 0
    mask = (col - row) * jnp.where(fwd, 1, -1) <= 0
    tri = jnp.where(mask, 1.0, 0.0)

    dt = dt_ref[...]
    dta = dt * (-jnp.exp(a_ref[...]))
    cum = jnp.dot(tri, dta, preferred_element_type=F32, precision=lax.Precision.HIGHEST)
    cum_t = cum.T
    tot = jnp.where(fwd, cum[q - 1:q, :], cum[0:1, :])
    exp_cum = jnp.exp(cum)
    dt_end = dt * jnp.exp(tot - cum)
    exp_tot = jnp.exp(tot)

    lane = lax.broadcasted_iota(jnp.int32, (q, 2 * p), 1)
    sub = lax.broadcasted_iota(jnp.int32, (2 * p, 1), 0)
    first = lane < p

    def pair_cols(mat, hd):
        return jnp.where(first, mat[:, hd:hd + 1], mat[:, hd + 1:hd + 2])

    for g in range(SSD_GROUPS):
        bm = xbc_ref[:, d_inner + g * SSD_STATE:d_inner + (g + 1) * SSD_STATE].astype(BF16)
        cm = xbc_ref[:, d_inner + gn + g * SSD_STATE:d_inner + gn + (g + 1) * SSD_STATE].astype(BF16)
        cb = lax.dot_general(cm, bm, (((1,), (1,)), ((), ())), preferred_element_type=F32)
        for pr in range(heads_per_group // 2):
            hd = g * heads_per_group + 2 * pr
            pair = hd // 2
            xs = xbc_ref[:, hd * p:(hd + 2) * p]
            xdt = xs * pair_cols(dt, hd)
            y = None
            for k in range(2):
                seg = cum[:, hd + k:hd + k + 1] - cum_t[hd + k:hd + k + 1, :]
                m = (jnp.where(mask, jnp.exp(seg), 0.0) * cb).astype(BF16)
                rhs = jnp.where(first if k == 0 else jnp.logical_not(first), xdt, 0.0).astype(BF16)
                yk = jnp.dot(m, rhs, preferred_element_type=F32)
                y = yk if y is None else y + yk
            h_prev = state_s[pair]
            y_off = lax.dot_general(cm, h_prev.astype(BF16), (((1,), (1,)), ((), ())),
                                    preferred_element_type=F32)
            y_ref[:, hd * p:(hd + 2) * p] = y + y_off * pair_cols(exp_cum, hd)
            xdtw = (xs * pair_cols(dt_end, hd)).astype(BF16)
            upd = lax.dot_general(xdtw, bm, (((0,), (0,)), ((), ())), preferred_element_type=F32)
            dec = jnp.where(sub < p, exp_tot[:, hd:hd + 1], exp_tot[:, hd + 1:hd + 2])
            state_s[pair] = dec * h_prev + upd

    @pl.when(jnp.logical_and(seq_last, is_ctx))
    def _():
        st_ref[...] = state_s[...]


def _ssd_scan(xbc, dt, a_log, h0, *, n_heads, n_ctx_seq, ctx_nc, lat_nc):
    t, dx = xbc.shape
    d_inner = n_heads * SSD_HEAD_DIM
    n_pairs = n_heads // 2
    q = SSD_CHUNK
    n_chunks = t // q
    ctx_chunks = n_ctx_seq * ctx_nc
    state_block = (None, None, n_pairs, 2 * SSD_HEAD_DIM, SSD_STATE)

    def chunk_of(d, c):
        return _scan_chunk(d, c, n_chunks)

    def h0_map(d, c):
        return (jnp.maximum(chunk_of(d, c) - ctx_chunks, 0) // lat_nc, d, 0, 0, 0)

    def st_map(d, c):
        return (jnp.minimum(chunk_of(d, c), ctx_chunks - 1) // ctx_nc, d, 0, 0, 0)

    return pl.pallas_call(
        functools.partial(_ssd_scan_kernel, n_heads, ctx_chunks, ctx_nc, lat_nc),
        grid=(2, n_chunks),
        in_specs=[
            pl.BlockSpec((q, dx), lambda d, c: (chunk_of(d, c), 0)),
            pl.BlockSpec((None, q, LANES), lambda d, c: (d, chunk_of(d, c), 0)),
            pl.BlockSpec((None, 1, LANES), lambda d, c: (d, 0, 0)),
            pl.BlockSpec(state_block, h0_map),
        ],
        out_specs=[
            pl.BlockSpec((None, q, d_inner), lambda d, c: (d, chunk_of(d, c), 0)),
            pl.BlockSpec(state_block, st_map),
        ],
        out_shape=[
            jax.ShapeDtypeStruct((2, t, d_inner), F32),
            jax.ShapeDtypeStruct((n_ctx_seq, 2, n_pairs, 2 * SSD_HEAD_DIM, SSD_STATE), F32),
        ],
        scratch_shapes=[pltpu.VMEM((n_pairs, 2 * SSD_HEAD_DIM, SSD_STATE), F32)],
        compiler_params=_cparams(("arbitrary", "arbitrary")),
        name="ssd_scan",
    )(xbc, dt, a_log, h0)


def _ssd_out_kernel(x_ref, mod_ref, y_ref, xs_ref, z_ref, dskip_ref, ng_ref, wout_ref, o_ref):
    y = y_ref[0] + y_ref[1] + dskip_ref[...] * xs_ref[...]
    yz = y * _silu(z_ref[...])
    ms = jnp.mean(yz * yz, axis=-1, keepdims=True)
    yn = (yz * lax.rsqrt(ms + EPS) * ng_ref[...]).astype(BF16)
    mix = jnp.dot(yn, wout_ref[...], preferred_element_type=F32)
    o_ref[...] = x_ref[...] + mod_ref[...][2:3] * mix


def _ssd_out(x, mod4, layer, y, xbc, z, d_skip, norm_g, w_out, *, tt, n_ctx_tiles,
             tiles_per_lat_seq):
    t, d = x.shape
    di = z.shape[1]
    const = lambda i: (0, 0)
    return pl.pallas_call(
        _ssd_out_kernel,
        grid=(t // tt,),
        in_specs=[
            pl.BlockSpec((tt, d), lambda i: (i, 0)),
            pl.BlockSpec((None, None, 6, d), _mod_row_map(layer, n_ctx_tiles, tiles_per_lat_seq)),
            pl.BlockSpec((2, tt, di), lambda i: (0, i, 0)),
            pl.BlockSpec((tt, di), lambda i: (i, 0)),
            pl.BlockSpec((tt, di), lambda i: (i, 0)),
            pl.BlockSpec((1, di), const),
            pl.BlockSpec((1, di), const),
            pl.BlockSpec((di, d), const),
        ],
        out_specs=pl.BlockSpec((tt, d), lambda i: (i, 0)),
        out_shape=jax.ShapeDtypeStruct((t, d), F32),
        compiler_params=_cparams(("arbitrary",)),
        name="ssd_out",
    )(x, mod4, y, xbc, z, d_skip, norm_g, w_out)


def _pick_tile(candidates, *lengths):
    for tt in candidates:
        if all(n % tt == 0 for n in lengths):
            return tt
    raise ValueError(f"no tile in {candidates} divides {lengths}")


def kernel(x_prompt, x_sample, state_ssm, c, c_ctx, norm_mix_g, norm_ffn_g, norm_f_g, ada_w, ada_b, sc_w_in, sc_conv_w, sc_w_out, ssd_w_in, ssd_conv_w, ssd_conv_b, ssd_dt_bias, ssd_a_log, ssd_d, ssd_norm_g, ssd_w_out, peer_wq, peer_keys, peer_u, peer_v):
    n_ctx_seq, ctx_len, d = x_prompt.shape
    n_lat_seq, lat_len, _ = x_sample.shape
    depth = ada_w.shape[0]
    t_ctx = n_ctx_seq * ctx_len
    d_inner = ssd_norm_g.shape[1]
    n_heads = d_inner // SSD_HEAD_DIM
    conv_dim = ssd_conv_w.shape[2]
    assert ctx_len & (ctx_len - 1) == 0 and ctx_len % SSD_CHUNK == 0 and lat_len % SSD_CHUNK == 0
    assert n_heads % (2 * SSD_GROUPS) == 0 and n_heads <= LANES

    x = jnp.concatenate([x_prompt.reshape(t_ctx, d), x_sample.reshape(n_lat_seq * lat_len, d)], axis=0)
    n_cond = 1 + n_lat_seq
    cond_rows = -(-n_cond // SUBLANES) * SUBLANES
    cond = jnp.concatenate([c_ctx[None], c, jnp.zeros((cond_rows - n_cond, d), F32)], axis=0)
    mod4 = _adaln(cond, ada_w, ada_b).reshape(depth, cond_rows, 6, d)

    def tiling(tt):
        return dict(tt=tt, n_ctx_tiles=t_ctx // tt, tiles_per_lat_seq=lat_len // tt)

    tt_mix = _pick_tile((512, 256), t_ctx, lat_len)
    tt_route = _pick_tile((ROUTE_TILE,), t_ctx, lat_len)
    tt_dense = _pick_tile((512, 256), t_ctx, lat_len)
    assert tt_mix % ctx_len == 0 and tt_mix % GRID_W == 0

    states = []
    for i in range(depth):
        j = i // 2
        if i % 2 == 0:
            x = _conv_mixer(x, mod4, i, norm_mix_g[i][None], sc_w_in[j].astype(BF16), sc_conv_w[j],
                            sc_w_out[j].astype(BF16), ctx_rowlen=ctx_len, **tiling(tt_mix))
        else:
            w_in = ssd_w_in[j]
            w_z = w_in[:, :d_inner].astype(BF16)
            w_x = w_in[:, d_inner:d_inner + conv_dim].astype(BF16)
            w_dt_raw = w_in[:, d_inner + conv_dim:]
            pad = jnp.zeros((d, LANES - n_heads), F32)
            w_dt = jnp.concatenate([w_dt_raw[:, :n_heads], pad, w_dt_raw[:, n_heads:], pad], axis=1).astype(BF16)
            lane_pad = ((0, 0), (0, LANES - n_heads))
            dt_bias = jnp.pad(ssd_dt_bias[j], lane_pad)
            a_log = jnp.pad(ssd_a_log[j], lane_pad)[:, None, :]
            z, xbc, dt = _ssd_in(x, mod4, i, norm_mix_g[i][None], w_z, w_x, w_dt, ssd_conv_w[j],
                                 ssd_conv_b[j][None], dt_bias, ctx_rowlen=ctx_len, **tiling(tt_mix))
            n_pairs = n_heads // 2
            h0 = state_ssm[:, j].reshape(n_lat_seq, 2, n_pairs, 2 * SSD_HEAD_DIM, SSD_STATE)
            y, st = _ssd_scan(xbc, dt, a_log, h0, n_heads=n_heads, n_ctx_seq=n_ctx_seq,
                              ctx_nc=ctx_len // SSD_CHUNK, lat_nc=lat_len // SSD_CHUNK)
            states.append(st.reshape(n_ctx_seq, 2, n_heads, SSD_HEAD_DIM, SSD_STATE))
            d_skip = jnp.repeat(ssd_d[j][0] + ssd_d[j][1], SSD_HEAD_DIM)[None]
            x = _ssd_out(x, mod4, i, y, xbc, z, d_skip, ssd_norm_g[j][None],
                         ssd_w_out[j].astype(BF16), **tiling(tt_mix))
        keys = peer_keys[i].reshape(2 * PEER_HEADS, N_KEYS, PEER_HALF).astype(BF16)
        i1, i2, gate = _peer_route(x, mod4, i, norm_ffn_g[i][None], peer_wq[i].astype(BF16), keys,
                                   **tiling(tt_route))
        x = _peer_dense(x, mod4, i, norm_ffn_g[i][None], i1, i2, gate, peer_u[i].astype(BF16),
                        peer_v[i].astype(BF16), norm_f_g[None], keys_per_blk=8,
                        final_norm=(i == depth - 1), **tiling(tt_dense))

    y_prompt = x[:t_ctx].reshape(n_ctx_seq, ctx_len, d)
    y_sample = x[t_ctx:].reshape(n_lat_seq, lat_len, d)
    return y_prompt, y_sample, jnp.stack(states, axis=1)
```

```python
import functools
import math

import jax
import jax.numpy as jnp
from jax import lax
from jax.experimental import pallas as pl
from jax.experimental.pallas import tpu as pltpu

F32 = jnp.float32
BF16 = jnp.bfloat16
EPS = 1e-6

LANES = 128
SUBLANES = 8
VMEM_LIMIT_BYTES = 60 * 1024 * 1024

GRID_W = 64
CONV_W = 3
SSD_HEAD_DIM = 64
SSD_STATE = 128
SSD_GROUPS = 8
SSD_CHUNK = 128
N_KEYS = 128
PEER_HEADS = 8
PEER_TOPK = 16
PEER_HALF = 128

NEG_INF = float("-inf")


def _cparams(sem):
    return pltpu.CompilerParams(dimension_semantics=sem, vmem_limit_bytes=VMEM_LIMIT_BYTES)


def _norm_mod(x, g, scl, sh):
    ms = jnp.mean(x * x, axis=-1, keepdims=True)
    return x * lax.rsqrt(ms + EPS) * g * (1.0 + scl) + sh


def _silu(x):
    return x * (1.0 / (1.0 + jnp.exp(-x)))


def _mod_row_map(layer, n_ctx_tiles, tiles_per_lat_seq):
    def index_map(i, *_):
        row = jnp.where(i < n_ctx_tiles, 0, 1 + (i - n_ctx_tiles) // tiles_per_lat_seq)
        return (layer, row, 0, 0)
    return index_map


def _shift_rows(u, pos, rowlen):
    n = u.shape[0]
    up = pltpu.roll(u, 1, axis=0)
    dn = pltpu.roll(u, n - 1, axis=0)
    up = jnp.where(pos == 0, 0.0, up)
    dn = jnp.where(pos == rowlen - 1, 0.0, dn)
    return up, dn


def _adaln_kernel(c_ref, w_ref, b_ref, o_ref):
    s = _silu(c_ref[...])
    o_ref[...] = jnp.dot(s, w_ref[...], preferred_element_type=F32,
                         precision=lax.Precision.HIGHEST) + b_ref[...]


def _adaln(cond, ada_w, ada_b):
    depth, d, n = ada_w.shape
    rows = cond.shape[0]
    tn = 1536
    return pl.pallas_call(
        _adaln_kernel,
        grid=(depth, n // tn),
        in_specs=[
            pl.BlockSpec((rows, d), lambda l, j: (0, 0)),
            pl.BlockSpec((None, d, tn), lambda l, j: (l, 0, j)),
            pl.BlockSpec((None, 1, tn), lambda l, j: (l, 0, j)),
        ],
        out_specs=pl.BlockSpec((None, rows, tn), lambda l, j: (l, 0, j)),
        out_shape=jax.ShapeDtypeStruct((depth, rows, n), F32),
        compiler_params=_cparams(("arbitrary", "arbitrary")),
        name="adaln",
    )(cond, ada_w, ada_b.reshape(depth, 1, n))


def _conv_mixer_kernel(n_ctx_tiles, ctx_rowlen, xc_ref, xl_ref, mod_ref, g_ref, win_ref, cw_ref,
                       wout_ref, o_ref):
    i = pl.program_id(0)
    x = jnp.where(i < n_ctx_tiles, xc_ref[...], xl_ref[...])
    d = x.shape[1]
    mod = mod_ref[...]
    hn = _norm_mod(x, g_ref[...], mod[1:2], mod[0:1]).astype(BF16)
    p = jnp.dot(hn, win_ref[...], preferred_element_type=F32)
    bg, cg, xv = p[:, :d], p[:, d:2 * d], p[:, 2 * d:]
    u = cg * xv
    rowlen = jnp.where(i < n_ctx_tiles, ctx_rowlen, GRID_W)
    pos = lax.broadcasted_iota(jnp.int32, (x.shape[0], 1), 0) & (rowlen - 1)
    up, dn = _shift_rows(u, pos, rowlen)
    cw = cw_ref[...]
    y = up * cw[0:1] + u * cw[1:2] + dn * cw[2:3]
    mix = jnp.dot((bg * y).astype(BF16), wout_ref[...], preferred_element_type=F32)
    o_ref[...] = x + mod[2:3] * mix


def _conv_mixer(x_ctx, x_lat, mod4, layer, g, w_in, conv_w, w_out, *, tt, n_ctx_tiles, ctx_rowlen,
                tiles_per_lat_seq):
    d = x_ctx.shape[1]
    t = x_ctx.shape[0] + x_lat.shape[0]
    return pl.pallas_call(
        functools.partial(_conv_mixer_kernel, n_ctx_tiles, ctx_rowlen),
        grid=(t // tt,),
        in_specs=[
            pl.BlockSpec((tt, d), lambda i: (jnp.minimum(i, n_ctx_tiles - 1), 0)),
            pl.BlockSpec((tt, d), lambda i: (jnp.maximum(i - n_ctx_tiles, 0), 0)),
            pl.BlockSpec((None, None, 6, d), _mod_row_map(layer, n_ctx_tiles, tiles_per_lat_seq)),
            pl.BlockSpec((1, d), lambda i: (0, 0)),
            pl.BlockSpec((d, 3 * d), lambda i: (0, 0)),
            pl.BlockSpec((CONV_W, d), lambda i: (0, 0)),
            pl.BlockSpec((d, d), lambda i: (0, 0)),
        ],
        out_specs=pl.BlockSpec((tt, d), lambda i: (i, 0)),
        out_shape=jax.ShapeDtypeStruct((t, d), F32),
        compiler_params=_cparams(("arbitrary",)),
        name="conv_mixer",
    )(x_ctx, x_lat, mod4, g, w_in, conv_w, w_out)


ROUTE_TILE = SUBLANES * LANES


def _oddeven_merge_sort_pairs(n):
    pairs = []
    p = 1
    while p < n:
        k = p
        while k >= 1:
            for j in range(k % p, n - k, 2 * k):
                for i in range(min(k, n - j - k)):
                    if (i + j) // (2 * p) == (i + j + k) // (2 * p):
                        pairs.append((i + j, i + j + k))
            k //= 2
        p *= 2
    return pairs


_SORT16 = _oddeven_merge_sort_pairs(PEER_TOPK)


def _goes_first(a, ia, b, ib):
    return (a > b) | ((a == b) & (ia < ib))


def _compare_exchange(v, ix, i, j, ids_ordered=False):
    a, b, ia, ib = v[i], v[j], ix[i], ix[j]
    f = (a >= b) if ids_ordered else _goes_first(a, ia, b, ib)
    v[i], v[j] = jnp.where(f, a, b), jnp.where(f, b, a)
    ix[i], ix[j] = jnp.where(f, ia, ib), jnp.where(f, ib, ia)


def _sort16(v, ix):
    for n, (i, j) in enumerate(_SORT16):
        _compare_exchange(v, ix, i, j, ids_ordered=n < PEER_TOPK // 2)


def _bitonic_sort16(v, ix):
    d = PEER_TOPK // 2
    while d >= 1:
        for i in range(PEER_TOPK):
            if i & d == 0:
                _compare_exchange(v, ix, i, i + d)
        d //= 2


def _merge_top16(rv, ri, xv, xi, sort=True):
    for k in range(len(xv)):
        i = PEER_TOPK - 1 - k
        f = _goes_first(rv[i], ri[i], xv[k], xi[k])
        rv[i] = jnp.where(f, rv[i], xv[k])
        ri[i] = jnp.where(f, ri[i], xi[k])
    if sort:
        _bitonic_sort16(rv, ri)


def _vreg_rows(k):
    return pl.ds(k * SUBLANES, SUBLANES)


def _peer_route_kernel(x_ref, mod_ref, g_ref, wq_ref, keys_ref, i1_ref, i2_ref, gate_ref,
                       hn_s, q_s, sc_s, topv_s, topi_s, e1_s, e2_s, gt_s):
    mod = mod_ref[...]
    hn_s[...] = _norm_mod(x_ref[...], g_ref[...], mod[4:5], mod[3:4]).astype(BF16)
    for h in range(PEER_HEADS):
        qh = jnp.dot(hn_s[...], wq_ref[:, 2 * h * PEER_HALF:2 * (h + 1) * PEER_HALF],
                     preferred_element_type=F32).astype(BF16)
        q_s[2 * h] = qh[:, :PEER_HALF]
        q_s[2 * h + 1] = qh[:, PEER_HALF:]

    def stage1(hp, carry):
        st = lax.dot_general(keys_ref[hp], q_s[hp], (((1,), (1,)), ((), ())),
                             preferred_element_type=F32)
        for c in range(SUBLANES):
            sc_s[pl.ds(c, N_KEYS, stride=SUBLANES), :] = st[:, c * LANES:(c + 1) * LANES]
        rv = ri = None
        for grp in range(N_KEYS // PEER_TOPK):
            v = [sc_s[_vreg_rows(grp * PEER_TOPK + i), :] for i in range(PEER_TOPK)]
            ix = [float(grp * PEER_TOPK + i) for i in range(PEER_TOPK)]
            _sort16(v, ix)
            if rv is None:
                rv, ri = v, ix
            else:
                _merge_top16(rv, ri, v, ix)
        for k in range(PEER_TOPK):
            topv_s[hp, _vreg_rows(k), :] = rv[k]
            topi_s[hp, _vreg_rows(k), :] = ri[k]
        return carry

    lax.fori_loop(0, 2 * PEER_HEADS, stage1, 0)

    n_wide = PEER_TOPK // 2
    cand_lists = [[(a, b) for b in range(PEER_TOPK // (a + 1))] for a in range(n_wide)]
    cand_lists.append([(a, 0) for a in range(n_wide, PEER_TOPK)])

    def stage2(h, carry):
        s1 = [topv_s[2 * h, _vreg_rows(k), :] for k in range(PEER_TOPK)]
        s2 = [topv_s[2 * h + 1, _vreg_rows(k), :] for k in range(PEER_TOPK)]
        rv = rf = None
        for n, pairs in enumerate(cand_lists):
            v = [s1[a] + s2[b] for a, b in pairs]
            flat = [float(a * PEER_TOPK + b) for a, b in pairs]
            if rv is None:
                rv, rf = v, flat
            else:
                _merge_top16(rv, rf, v, flat, sort=n < len(cand_lists) - 1)
        top = s1[0] + s2[0]
        ex = [jnp.exp(v - top) for v in rv]
        denom = ex[0]
        for e in ex[1:]:
            denom = denom + e
        inv = 1.0 / denom
        k1 = [topi_s[2 * h, _vreg_rows(k), :] for k in range(PEER_TOPK)]
        k2 = [topi_s[2 * h + 1, _vreg_rows(k), :] for k in range(PEER_TOPK)]
        for k in range(PEER_TOPK):
            pos1 = jnp.floor(rf[k] * (1.0 / PEER_TOPK))
            pos2 = rf[k] - pos1 * PEER_TOPK
            e1 = jnp.zeros_like(top)
            e2 = jnp.zeros_like(top)
            for a in range(PEER_TOPK):
                e1 = jnp.where(pos1 == a, k1[a], e1)
                e2 = jnp.where(pos2 == a, k2[a], e2)
            rows = pl.ds(pl.multiple_of((h * PEER_TOPK + k) * SUBLANES, SUBLANES), SUBLANES)
            e1_s[rows, :] = e1
            e2_s[rows, :] = e2
            gt_s[rows, :] = ex[k] * inv
        return carry

    lax.fori_loop(0, PEER_HEADS, stage2, 0)

    for src, dst in ((e1_s, i1_ref), (e2_s, i2_ref), (gt_s, gate_ref)):
        for c in range(SUBLANES):
            dst[c * LANES:(c + 1) * LANES, :] = src[pl.ds(c, LANES, stride=SUBLANES), :].T


def _peer_route(x, mod4, layer, g, wq, keys, *, tt, n_ctx_tiles, tiles_per_lat_seq):
    t, d = x.shape
    nsel = PEER_HEADS * PEER_TOPK
    assert tt == ROUTE_TILE and nsel == LANES
    out = jax.ShapeDtypeStruct((t, nsel), F32)
    vreg_table = pltpu.VMEM((nsel * SUBLANES, LANES), F32)
    return pl.pallas_call(
        _peer_route_kernel,
        grid=(t // tt,),
        scratch_shapes=[
            pltpu.VMEM((tt, d), BF16),
            pltpu.VMEM((2 * PEER_HEADS, tt, PEER_HALF), BF16),
            pltpu.VMEM((N_KEYS * SUBLANES, LANES), F32),
            pltpu.VMEM((2 * PEER_HEADS, PEER_TOPK * SUBLANES, LANES), F32),
            pltpu.VMEM((2 * PEER_HEADS, PEER_TOPK * SUBLANES, LANES), F32),
            vreg_table, vreg_table, vreg_table,
        ],
        in_specs=[
            pl.BlockSpec((tt, d), lambda i: (i, 0)),
            pl.BlockSpec((None, None, 6, d), _mod_row_map(layer, n_ctx_tiles, tiles_per_lat_seq)),
            pl.BlockSpec((1, d), lambda i: (0, 0)),
            pl.BlockSpec(wq.shape, lambda i: (0, 0)),
            pl.BlockSpec(keys.shape, lambda i: (0, 0, 0)),
        ],
        out_specs=[pl.BlockSpec((tt, nsel), lambda i: (i, 0))] * 3,
        out_shape=[out, out, out],
        compiler_params=_cparams(("arbitrary",)),
        name="peer_route",
    )(x, mod4, g, wq, keys)


def _gate_pitch(tt):
    return tt + SUBLANES


def _peer_dense_kernel(final, tt, keys_per_blk, n_ctx_tiles, x_ref, mod_ref, g_ref, i1_ref, i2_ref,
                       gate_ref, u_ref, v_ref, gf_ref, *refs):
    *out_refs, hn_s, w_s, acc_s = refs
    eb = pl.program_id(1)
    pitch = _gate_pitch(tt)

    @pl.when(eb == 0)
    def _():
        mod = mod_ref[...]
        hn_s[...] = _norm_mod(x_ref[...], g_ref[...], mod[4:5], mod[3:4]).astype(BF16)
        acc_s[...] = jnp.zeros_like(acc_s)
        key_ids = lax.broadcasted_iota(jnp.int32, (N_KEYS, N_KEYS), 0).astype(F32)

        def build(t, carry):
            i1 = i1_ref[pl.ds(t, 1), :]
            i2 = i2_ref[pl.ds(t, 1), :]
            gt = gate_ref[pl.ds(t, 1), :]
            p = jnp.where(i1 == key_ids, gt, 0.0).astype(BF16)
            q = jnp.where(i2 == key_ids, 1.0, 0.0).astype(BF16)
            w = lax.dot_general(p, q, (((1,), (1,)), ((), ())), preferred_element_type=F32)
            w_s[pl.ds(t, N_KEYS, stride=pitch), :] = w
            return carry

        lax.fori_loop(0, tt, build, 0, unroll=32)

    act = lax.dot_general(hn_s[...], u_ref[...], (((1,), (1,)), ((), ())),
                          preferred_element_type=F32)
    hs = []
    for a in range(keys_per_blk):
        start = pl.multiple_of((eb * keys_per_blk + a) * pitch, SUBLANES)
        wa = w_s[pl.ds(start, tt), :]
        xa = act[:, a * N_KEYS:(a + 1) * N_KEYS]
        gelu = 0.5 * xa * (1.0 + lax.erf(xa * math.sqrt(0.5)))
        hs.append((gelu * wa).astype(BF16))
    h = jnp.concatenate(hs, axis=1)
    acc_s[...] += jnp.dot(h, v_ref[...], preferred_element_type=F32)

    @pl.when(eb == pl.num_programs(1) - 1)
    def _():
        y = x_ref[...] + mod_ref[...][5:6] * acc_s[...]
        if not final:
            out_refs[0][...] = y
        else:
            ms = jnp.mean(y * y, axis=-1, keepdims=True)
            y = y * lax.rsqrt(ms + EPS) * gf_ref[...]
            is_ctx = pl.program_id(0) < n_ctx_tiles

            @pl.when(is_ctx)
            def _():
                out_refs[0][...] = y

            @pl.when(jnp.logical_not(is_ctx))
            def _():
                out_refs[1][...] = y


def _peer_dense(x, mod4, layer, g, i1, i2, gate, u, v, g_final, *, tt, keys_per_blk,
                n_ctx_tiles, tiles_per_lat_seq, final_norm):
    t, d = x.shape
    n_exp = u.shape[0]
    eblk = keys_per_blk * N_KEYS
    nsel = i1.shape[1]
    pitch = _gate_pitch(tt)
    mod_map = _mod_row_map(layer, n_ctx_tiles, tiles_per_lat_seq)
    if final_norm:
        t_ctx = n_ctx_tiles * tt
        out_specs = [pl.BlockSpec((tt, d), lambda i, e: (jnp.minimum(i, n_ctx_tiles - 1), 0)),
                     pl.BlockSpec((tt, d), lambda i, e: (jnp.maximum(i - n_ctx_tiles, 0), 0))]
        out_shape = [jax.ShapeDtypeStruct((t_ctx, d), F32), jax.ShapeDtypeStruct((t - t_ctx, d), F32)]
    else:
        out_specs = pl.BlockSpec((tt, d), lambda i, e: (i, 0))
        out_shape = jax.ShapeDtypeStruct((t, d), F32)
    return pl.pallas_call(
        functools.partial(_peer_dense_kernel, final_norm, tt, keys_per_blk, n_ctx_tiles),
        grid=(t // tt, n_exp // eblk),
        in_specs=[
            pl.BlockSpec((tt, d), lambda i, e: (i, 0), pipeline_mode=pl.Buffered(1)),
            pl.BlockSpec((None, None, 6, d), mod_map),
            pl.BlockSpec((1, d), lambda i, e: (0, 0)),
            pl.BlockSpec((tt, nsel), lambda i, e: (i, 0), pipeline_mode=pl.Buffered(1)),
            pl.BlockSpec((tt, nsel), lambda i, e: (i, 0), pipeline_mode=pl.Buffered(1)),
            pl.BlockSpec((tt, nsel), lambda i, e: (i, 0), pipeline_mode=pl.Buffered(1)),
            pl.BlockSpec((eblk, d), lambda i, e: (e, 0)),
            pl.BlockSpec((eblk, d), lambda i, e: (e, 0)),
            pl.BlockSpec((1, d), lambda i, e: (0, 0)),
        ],
        out_specs=out_specs,
        out_shape=out_shape,
        scratch_shapes=[
            pltpu.VMEM((tt, d), BF16),
            pltpu.VMEM((N_KEYS * pitch, N_KEYS), F32),
            pltpu.VMEM((tt, d), F32),
        ],
        compiler_params=_cparams(("arbitrary", "arbitrary")),
        name="peer_dense",
    )(x, mod4, g, i1, i2, gate, u, v, g_final)


def _softplus(x):
    return jnp.maximum(x, 0.0) + jnp.log(1.0 + jnp.exp(-jnp.abs(x)))


def _ssd_in_kernel(n_ctx_tiles, ctx_rowlen, x_ref, mod_ref, g_ref, wz_ref, wx_ref, wdt_ref,
                   cw_ref, cb_ref, dtb_ref, z_ref, xbc_ref, dt_ref):
    i = pl.program_id(0)
    x = x_ref[...]
    mod = mod_ref[...]
    hn = _norm_mod(x, g_ref[...], mod[1:2], mod[0:1]).astype(BF16)
    z_ref[...] = jnp.dot(hn, wz_ref[...], preferred_element_type=F32)
    rowlen = jnp.where(i < n_ctx_tiles, ctx_rowlen, GRID_W)
    pos = lax.broadcasted_iota(jnp.int32, (x.shape[0], 1), 0) & (rowlen - 1)
    chunk = 4 * LANES
    for c0 in range(0, xbc_ref.shape[1], chunk):
        cols = slice(c0, c0 + chunk)
        xbc = jnp.dot(hn, wx_ref[:, cols], preferred_element_type=F32)
        up, dn = _shift_rows(xbc, pos, rowlen)
        cw = cw_ref[:, cols]
        xbc_ref[:, cols] = _silu(up * cw[0:1] + xbc * cw[1:2] + dn * cw[2:3] + cb_ref[:, cols])
    dt = jnp.dot(hn, wdt_ref[...], preferred_element_type=F32)
    dtb = dtb_ref[...]
    dt_ref[0] = _softplus(dt[:, :LANES] + dtb[0:1])
    dt_ref[1] = _softplus(dt[:, LANES:] + dtb[1:2])


def _ssd_in(x, mod4, layer, g, w_z, w_x, w_dt, conv_w, conv_b, dt_bias, *, tt, n_ctx_tiles,
            ctx_rowlen, tiles_per_lat_seq):
    t, d = x.shape
    dz, dx = w_z.shape[1], w_x.shape[1]
    const = lambda i: (0, 0)
    return pl.pallas_call(
        functools.partial(_ssd_in_kernel, n_ctx_tiles, ctx_rowlen),
        grid=(t // tt,),
        in_specs=[
            pl.BlockSpec((tt, d), lambda i: (i, 0)),
            pl.BlockSpec((None, None, 6, d), _mod_row_map(layer, n_ctx_tiles, tiles_per_lat_seq)),
            pl.BlockSpec((1, d), const),
            pl.BlockSpec((d, dz), const),
            pl.BlockSpec((d, dx), const),
            pl.BlockSpec((d, 2 * LANES), const),
            pl.BlockSpec((CONV_W, dx), const),
            pl.BlockSpec((1, dx), const),
            pl.BlockSpec((2, LANES), const),
        ],
        out_specs=[
            pl.BlockSpec((tt, dz), lambda i: (i, 0)),
            pl.BlockSpec((tt, dx), lambda i: (i, 0)),
            pl.BlockSpec((2, tt, LANES), lambda i: (0, i, 0)),
        ],
        out_shape=[
            jax.ShapeDtypeStruct((t, dz), F32),
            jax.ShapeDtypeStruct((t, dx), F32),
            jax.ShapeDtypeStruct((2, t, LANES), F32),
        ],
        compiler_params=_cparams(("arbitrary",)),
        name="ssd_in",
    )(x, mod4, g, w_z, w_x, w_dt, conv_w, conv_b, dt_bias)


def _scan_chunk(direction, step, n_chunks):
    return jnp.where(direction == 0, step, n_chunks - 1 - step)


def _ssd_scan_kernel(n_heads, ctx_chunks, ctx_nc, lat_nc, xbc_ref, dt_ref, a_ref, e_ref, h0_ref,
                     y_ref, st_ref, state_s):
    direction = pl.program_id(0)
    chunk = _scan_chunk(direction, pl.program_id(1), pl.num_programs(1))
    q = SSD_CHUNK
    p = SSD_HEAD_DIM
    d_inner = n_heads * p
    gn = SSD_GROUPS * SSD_STATE
    heads_per_group = n_heads // SSD_GROUPS

    is_ctx = chunk < ctx_chunks
    seq_nc = jnp.where(is_ctx, ctx_nc, lat_nc)
    local = jnp.where(is_ctx, lax.rem(chunk, ctx_nc), lax.rem(chunk - ctx_chunks, lat_nc))
    seq_first = local == jnp.where(direction == 0, 0, seq_nc - 1)
    seq_last = local == jnp.where(direction == 0, seq_nc - 1, 0)

    @pl.when(jnp.logical_and(seq_first, is_ctx))
    def _():
        state_s[...] = jnp.zeros_like(state_s)

    @pl.when(jnp.logical_and(seq_first, jnp.logical_not(is_ctx)))
    def _():
        for pair in range(n_heads // 2):
            state_s[pair] = h0_ref[pair].T

    row = lax.broadcasted_iota(jnp.int32, (q, q), 0)
    col = lax.broadcasted_iota(jnp.int32, (q, q), 1)
    fwd = direction == 0
    mask = (col - row) * jnp.where(fwd, 1, -1) <= 0
    tri = jnp.where(mask, 1.0, 0.0)

    dt = dt_ref[...]
    dta = dt * (-jnp.exp(a_ref[...]))
    cum = jnp.dot(tri, dta, preferred_element_type=F32, precision=lax.Precision.HIGHEST)
    cum_t = cum.T
    tot = jnp.where(fwd, cum[q - 1:q, :], cum[0:1, :])

    def per_lane(m):
        e = e_ref[...]
        hi = m.astype(BF16)
        lo = (m - hi.astype(F32)).astype(BF16)
        return jnp.dot(hi, e, preferred_element_type=F32) + jnp.dot(lo, e, preferred_element_type=F32)

    exp_cum = jnp.exp(cum)
    dt_l = per_lane(dt)
    dt_end_l = per_lane(dt * jnp.exp(tot - cum))
    exp_cum_l = per_lane(exp_cum)
    exp_tot_l = jnp.where(fwd, exp_cum_l[q - 1:q, :], exp_cum_l[0:1, :])

    first = lax.broadcasted_iota(jnp.int32, (q, 2 * p), 1) < p

    for g in range(SSD_GROUPS):
        bm_f = xbc_ref[:, d_inner + g * SSD_STATE:d_inner + (g + 1) * SSD_STATE]
        bm = bm_f.astype(BF16)
        bm_t = bm_f.T.astype(BF16)
        cm = xbc_ref[:, d_inner + gn + g * SSD_STATE:d_inner + gn + (g + 1) * SSD_STATE].astype(BF16)
        cb = lax.dot_general(cm, bm, (((1,), (1,)), ((), ())), preferred_element_type=F32)
        for pr in range(heads_per_group // 2):
            hd = g * heads_per_group + 2 * pr
            pair = hd // 2
            lanes = slice(hd * p, (hd + 2) * p)
            xs = xbc_ref[:, lanes]
            xdt = xs * dt_l[:, lanes]
            ms = []
            for k in range(2):
                seg = cum[:, hd + k:hd + k + 1] - cum_t[hd + k:hd + k + 1, :]
                ms.append((jnp.where(mask, jnp.exp(seg), 0.0) * cb).astype(BF16))
            rhs = jnp.concatenate([jnp.where(first, xdt, 0.0), jnp.where(first, 0.0, xdt)],
                                  axis=0).astype(BF16)
            y = jnp.dot(jnp.concatenate(ms, axis=1), rhs, preferred_element_type=F32)
            h_prev = state_s[pair]
            y_off = jnp.dot(cm, h_prev.astype(BF16), preferred_element_type=F32)
            y_ref[:, lanes] = y + y_off * exp_cum_l[:, lanes]
            xdtw = (xs * dt_end_l[:, lanes]).astype(BF16)
            upd = jnp.dot(bm_t, xdtw, preferred_element_type=F32)
            state_s[pair] = exp_tot_l[:, lanes] * h_prev + upd

    @pl.when(jnp.logical_and(seq_last, is_ctx))
    def _():
        for pair in range(n_heads // 2):
            st_ref[pair] = state_s[pair].T


def _ssd_scan(xbc, dt, a_log, h0, *, n_heads, n_ctx_seq, ctx_nc, lat_nc):
    t, dx = xbc.shape
    d_inner = n_heads * SSD_HEAD_DIM
    n_pairs = n_heads // 2
    q = SSD_CHUNK
    n_chunks = t // q
    ctx_chunks = n_ctx_seq * ctx_nc
    state_block = (None, None, n_pairs, 2 * SSD_HEAD_DIM, SSD_STATE)
    assert 2 * SSD_HEAD_DIM == SSD_STATE
    head_lanes = (jnp.arange(d_inner)[None, :] // SSD_HEAD_DIM == jnp.arange(LANES)[:, None]).astype(BF16)

    def chunk_of(d, c):
        return _scan_chunk(d, c, n_chunks)

    def h0_map(d, c):
        return (jnp.maximum(chunk_of(d, c) - ctx_chunks, 0) // lat_nc, d, 0, 0, 0)

    def st_map(d, c):
        return (jnp.minimum(chunk_of(d, c), ctx_chunks - 1) // ctx_nc, d, 0, 0, 0)

    return pl.pallas_call(
        functools.partial(_ssd_scan_kernel, n_heads, ctx_chunks, ctx_nc, lat_nc),
        grid=(2, n_chunks),
        in_specs=[
            pl.BlockSpec((q, dx), lambda d, c: (chunk_of(d, c), 0)),
            pl.BlockSpec((None, q, LANES), lambda d, c: (d, chunk_of(d, c), 0)),
            pl.BlockSpec((None, 1, LANES), lambda d, c: (d, 0, 0)),
            pl.BlockSpec((LANES, d_inner), lambda d, c: (0, 0)),
            pl.BlockSpec(state_block, h0_map),
        ],
        out_specs=[
            pl.BlockSpec((None, q, d_inner), lambda d, c: (d, chunk_of(d, c), 0)),
            pl.BlockSpec(state_block, st_map),
        ],
        out_shape=[
            jax.ShapeDtypeStruct((2, t, d_inner), F32),
            jax.ShapeDtypeStruct((n_ctx_seq, 2, n_pairs, 2 * SSD_HEAD_DIM, SSD_STATE), F32),
        ],
        scratch_shapes=[pltpu.VMEM((n_pairs, 2 * SSD_HEAD_DIM, SSD_STATE), F32)],
        compiler_params=_cparams(("arbitrary", "arbitrary")),
        name="ssd_scan",
    )(xbc, dt, a_log, head_lanes, h0)


def _ssd_out_kernel(x_ref, mod_ref, y_ref, xs_ref, z_ref, dskip_ref, ng_ref, wout_ref, o_ref):
    y = y_ref[0] + y_ref[1] + dskip_ref[...] * xs_ref[...]
    yz = y * _silu(z_ref[...])
    ms = jnp.mean(yz * yz, axis=-1, keepdims=True)
    yn = (yz * lax.rsqrt(ms + EPS) * ng_ref[...]).astype(BF16)
    mix = jnp.dot(yn, wout_ref[...], preferred_element_type=F32)
    o_ref[...] = x_ref[...] + mod_ref[...][2:3] * mix


def _ssd_out(x, mod4, layer, y, xbc, z, d_skip, norm_g, w_out, *, tt, n_ctx_tiles,
             tiles_per_lat_seq):
    t, d = x.shape
    di = z.shape[1]
    const = lambda i: (0, 0)
    return pl.pallas_call(
        _ssd_out_kernel,
        grid=(t // tt,),
        in_specs=[
            pl.BlockSpec((tt, d), lambda i: (i, 0)),
            pl.BlockSpec((None, None, 6, d), _mod_row_map(layer, n_ctx_tiles, tiles_per_lat_seq)),
            pl.BlockSpec((2, tt, di), lambda i: (0, i, 0)),
            pl.BlockSpec((tt, di), lambda i: (i, 0)),
            pl.BlockSpec((tt, di), lambda i: (i, 0)),
            pl.BlockSpec((1, di), const),
            pl.BlockSpec((1, di), const),
            pl.BlockSpec((di, d), const),
        ],
        out_specs=pl.BlockSpec((tt, d), lambda i: (i, 0)),
        out_shape=jax.ShapeDtypeStruct((t, d), F32),
        compiler_params=_cparams(("arbitrary",)),
        name="ssd_out",
    )(x, mod4, y, xbc, z, d_skip, norm_g, w_out)


def _pick_tile(candidates, *lengths):
    for tt in candidates:
        if all(n % tt == 0 for n in lengths):
            return tt
    raise ValueError(f"no tile in {candidates} divides {lengths}")


def kernel(x_prompt, x_sample, state_ssm, c, c_ctx, norm_mix_g, norm_ffn_g, norm_f_g, ada_w, ada_b, sc_w_in, sc_conv_w, sc_w_out, ssd_w_in, ssd_conv_w, ssd_conv_b, ssd_dt_bias, ssd_a_log, ssd_d, ssd_norm_g, ssd_w_out, peer_wq, peer_keys, peer_u, peer_v):
    n_ctx_seq, ctx_len, d = x_prompt.shape
    n_lat_seq, lat_len, _ = x_sample.shape
    depth = ada_w.shape[0]
    t_ctx = n_ctx_seq * ctx_len
    d_inner = ssd_norm_g.shape[1]
    n_heads = d_inner // SSD_HEAD_DIM
    conv_dim = ssd_conv_w.shape[2]
    assert ctx_len & (ctx_len - 1) == 0 and ctx_len % SSD_CHUNK == 0 and lat_len % SSD_CHUNK == 0
    assert n_heads % (2 * SSD_GROUPS) == 0 and n_heads <= LANES

    x = None
    n_cond = 1 + n_lat_seq
    cond_rows = -(-n_cond // SUBLANES) * SUBLANES
    cond = jnp.concatenate([c_ctx[None], c, jnp.zeros((cond_rows - n_cond, d), F32)], axis=0)
    mod4 = _adaln(cond, ada_w, ada_b).reshape(depth, cond_rows, 6, d)

    def tiling(tt):
        return dict(tt=tt, n_ctx_tiles=t_ctx // tt, tiles_per_lat_seq=lat_len // tt)

    tt_mix = _pick_tile((512, 256), t_ctx, lat_len)
    tt_route = _pick_tile((ROUTE_TILE,), t_ctx, lat_len)
    tt_dense = _pick_tile((512, 256), t_ctx, lat_len)
    assert tt_mix % ctx_len == 0 and tt_mix % GRID_W == 0

    states = []
    for i in range(depth):
        j = i // 2
        if i % 2 == 0:
            if x is None:
                x_ctx, x_lat = x_prompt.reshape(t_ctx, d), x_sample.reshape(n_lat_seq * lat_len, d)
            else:
                x_ctx, x_lat = x[:t_ctx], x[t_ctx:]
            x = _conv_mixer(x_ctx, x_lat, mod4, i, norm_mix_g[i][None], sc_w_in[j].astype(BF16), sc_conv_w[j],
                            sc_w_out[j].astype(BF16), ctx_rowlen=ctx_len, **tiling(tt_mix))
        else:
            w_in = ssd_w_in[j]
            w_z = w_in[:, :d_inner].astype(BF16)
            w_x = w_in[:, d_inner:d_inner + conv_dim].astype(BF16)
            w_dt_raw = w_in[:, d_inner + conv_dim:]
            pad = jnp.zeros((d, LANES - n_heads), F32)
            w_dt = jnp.concatenate([w_dt_raw[:, :n_heads], pad, w_dt_raw[:, n_heads:], pad], axis=1).astype(BF16)
            lane_pad = ((0, 0), (0, LANES - n_heads))
            dt_bias = jnp.pad(ssd_dt_bias[j], lane_pad)
            a_log = jnp.pad(ssd_a_log[j], lane_pad)[:, None, :]
            z, xbc, dt = _ssd_in(x, mod4, i, norm_mix_g[i][None], w_z, w_x, w_dt, ssd_conv_w[j],
                                 ssd_conv_b[j][None], dt_bias, ctx_rowlen=ctx_len, **tiling(tt_mix))
            n_pairs = n_heads // 2
            h0 = state_ssm[:, j].reshape(n_lat_seq, 2, n_pairs, 2 * SSD_HEAD_DIM, SSD_STATE)
            y, st = _ssd_scan(xbc, dt, a_log, h0, n_heads=n_heads, n_ctx_seq=n_ctx_seq,
                              ctx_nc=ctx_len // SSD_CHUNK, lat_nc=lat_len // SSD_CHUNK)
            states.append(st.reshape(n_ctx_seq, 2, n_heads, SSD_HEAD_DIM, SSD_STATE))
            d_skip = jnp.repeat(ssd_d[j][0] + ssd_d[j][1], SSD_HEAD_DIM)[None]
            x = _ssd_out(x, mod4, i, y, xbc, z, d_skip, ssd_norm_g[j][None],
                         ssd_w_out[j].astype(BF16), **tiling(tt_mix))
        keys = peer_keys[i].reshape(2 * PEER_HEADS, N_KEYS, PEER_HALF).astype(BF16)
        i1, i2, gate = _peer_route(x, mod4, i, norm_ffn_g[i][None], peer_wq[i].astype(BF16), keys,
                                   **tiling(tt_route))
        x = _peer_dense(x, mod4, i, norm_ffn_g[i][None], i1, i2, gate, peer_u[i].astype(BF16),
                        peer_v[i].astype(BF16), norm_f_g[None], keys_per_blk=8,
                        final_norm=(i == depth - 1), **tiling(tt_dense))

    y_ctx, y_lat = x
    return (y_ctx.reshape(n_ctx_seq, ctx_len, d), y_lat.reshape(n_lat_seq, lat_len, d),
            jnp.stack(states, axis=1))
```

```python
import functools
import math

import jax
import jax.numpy as jnp
from jax import lax
from jax.experimental import pallas as pl
from jax.experimental.pallas import tpu as pltpu

F32 = jnp.float32
BF16 = jnp.bfloat16
EPS = 1e-6

LANES = 128
SUBLANES = 8
VMEM_LIMIT_BYTES = 60 * 1024 * 1024

GRID_W = 64
CONV_W = 3
SSD_HEAD_DIM = 64
SSD_STATE = 128
SSD_GROUPS = 8
SSD_CHUNK = 128
N_KEYS = 128
PEER_HEADS = 8
PEER_TOPK = 16
PEER_HALF = 128

NEG_INF = float("-inf")


def _cparams(sem):
    return pltpu.CompilerParams(dimension_semantics=sem, vmem_limit_bytes=VMEM_LIMIT_BYTES)


def _norm_mod(x, g, scl, sh):
    ms = jnp.mean(x * x, axis=-1, keepdims=True)
    return x * lax.rsqrt(ms + EPS) * g * (1.0 + scl) + sh


def _silu(x):
    return x * (1.0 / (1.0 + jnp.exp(-x)))


def _mod_row_map(layer, n_ctx_tiles, tiles_per_lat_seq):
    def index_map(i, *_):
        row = jnp.where(i < n_ctx_tiles, 0, 1 + (i - n_ctx_tiles) // tiles_per_lat_seq)
        return (layer, row, 0, 0)
    return index_map


def _shift_rows(u, pos, rowlen):
    n = u.shape[0]
    up = pltpu.roll(u, 1, axis=0)
    dn = pltpu.roll(u, n - 1, axis=0)
    up = jnp.where(pos == 0, 0.0, up)
    dn = jnp.where(pos == rowlen - 1, 0.0, dn)
    return up, dn


def _adaln_kernel(c_ref, w_ref, b_ref, o_ref):
    s = _silu(c_ref[...])
    o_ref[...] = jnp.dot(s, w_ref[...], preferred_element_type=F32,
                         precision=lax.Precision.HIGHEST) + b_ref[...]


def _adaln(cond, ada_w, ada_b):
    depth, d, n = ada_w.shape
    rows = cond.shape[0]
    tn = 1536
    return pl.pallas_call(
        _adaln_kernel,
        grid=(depth, n // tn),
        in_specs=[
            pl.BlockSpec((rows, d), lambda l, j: (0, 0)),
            pl.BlockSpec((None, d, tn), lambda l, j: (l, 0, j)),
            pl.BlockSpec((None, 1, tn), lambda l, j: (l, 0, j)),
        ],
        out_specs=pl.BlockSpec((None, rows, tn), lambda l, j: (l, 0, j)),
        out_shape=jax.ShapeDtypeStruct((depth, rows, n), F32),
        compiler_params=_cparams(("arbitrary", "arbitrary")),
        name="adaln",
    )(cond, ada_w, ada_b.reshape(depth, 1, n))


def _conv_mixer_kernel(n_ctx_tiles, ctx_rowlen, xc_ref, xl_ref, mod_ref, g_ref, win_ref, cw_ref,
                       wout_ref, o_ref):
    i = pl.program_id(0)
    x = jnp.where(i < n_ctx_tiles, xc_ref[...], xl_ref[...])
    d = x.shape[1]
    mod = mod_ref[...]
    hn = _norm_mod(x, g_ref[...], mod[1:2], mod[0:1]).astype(BF16)
    p = jnp.dot(hn, win_ref[...], preferred_element_type=F32)
    bg, cg, xv = p[:, :d], p[:, d:2 * d], p[:, 2 * d:]
    u = cg * xv
    rowlen = jnp.where(i < n_ctx_tiles, ctx_rowlen, GRID_W)
    pos = lax.broadcasted_iota(jnp.int32, (x.shape[0], 1), 0) & (rowlen - 1)
    up, dn = _shift_rows(u, pos, rowlen)
    cw = cw_ref[...]
    y = up * cw[0:1] + u * cw[1:2] + dn * cw[2:3]
    mix = jnp.dot((bg * y).astype(BF16), wout_ref[...], preferred_element_type=F32)
    o_ref[...] = x + mod[2:3] * mix


def _conv_mixer(x_ctx, x_lat, mod4, layer, g, w_in, conv_w, w_out, *, tt, n_ctx_tiles, ctx_rowlen,
                tiles_per_lat_seq):
    d = x_ctx.shape[1]
    t = x_ctx.shape[0] + x_lat.shape[0]
    return pl.pallas_call(
        functools.partial(_conv_mixer_kernel, n_ctx_tiles, ctx_rowlen),
        grid=(t // tt,),
        in_specs=[
            pl.BlockSpec((tt, d), lambda i: (jnp.minimum(i, n_ctx_tiles - 1), 0)),
            pl.BlockSpec((tt, d), lambda i: (jnp.maximum(i - n_ctx_tiles, 0), 0)),
            pl.BlockSpec((None, None, 6, d), _mod_row_map(layer, n_ctx_tiles, tiles_per_lat_seq)),
            pl.BlockSpec((1, d), lambda i: (0, 0)),
            pl.BlockSpec((d, 3 * d), lambda i: (0, 0)),
            pl.BlockSpec((CONV_W, d), lambda i: (0, 0)),
            pl.BlockSpec((d, d), lambda i: (0, 0)),
        ],
        out_specs=pl.BlockSpec((tt, d), lambda i: (i, 0)),
        out_shape=jax.ShapeDtypeStruct((t, d), F32),
        compiler_params=_cparams(("arbitrary",)),
        name="conv_mixer",
    )(x_ctx, x_lat, mod4, g, w_in, conv_w, w_out)


ROUTE_TILE = SUBLANES * LANES


def _oddeven_merge_sort_pairs(n):
    pairs = []
    p = 1
    while p < n:
        k = p
        while k >= 1:
            for j in range(k % p, n - k, 2 * k):
                for i in range(min(k, n - j - k)):
                    if (i + j) // (2 * p) == (i + j + k) // (2 * p):
                        pairs.append((i + j, i + j + k))
            k //= 2
        p *= 2
    return pairs


_SORT16 = _oddeven_merge_sort_pairs(PEER_TOPK)


def _goes_first(a, ia, b, ib):
    return (a > b) | ((a == b) & (ia < ib))


def _compare_exchange(v, ix, i, j, ids_ordered=False):
    a, b, ia, ib = v[i], v[j], ix[i], ix[j]
    f = (a >= b) if ids_ordered else _goes_first(a, ia, b, ib)
    v[i], v[j] = jnp.where(f, a, b), jnp.where(f, b, a)
    ix[i], ix[j] = jnp.where(f, ia, ib), jnp.where(f, ib, ia)


def _sort16(v, ix):
    for n, (i, j) in enumerate(_SORT16):
        _compare_exchange(v, ix, i, j, ids_ordered=n < PEER_TOPK // 2)


def _bitonic_sort16(v, ix):
    d = PEER_TOPK // 2
    while d >= 1:
        for i in range(PEER_TOPK):
            if i & d == 0:
                _compare_exchange(v, ix, i, i + d)
        d //= 2


def _merge_top16(rv, ri, xv, xi, sort=True):
    for k in range(len(xv)):
        i = PEER_TOPK - 1 - k
        f = _goes_first(rv[i], ri[i], xv[k], xi[k])
        rv[i] = jnp.where(f, rv[i], xv[k])
        ri[i] = jnp.where(f, ri[i], xi[k])
    if sort:
        _bitonic_sort16(rv, ri)


def _vreg_rows(k):
    return pl.ds(k * SUBLANES, SUBLANES)


def _peer_route_kernel(x_ref, mod_ref, g_ref, wq_ref, keys_ref, i1_ref, i2_ref, gate_ref,
                       hn_s, q_s, sc_s, topv_s, topi_s, e1_s, e2_s, gt_s):
    mod = mod_ref[...]
    hn_s[...] = _norm_mod(x_ref[...], g_ref[...], mod[4:5], mod[3:4]).astype(BF16)
    for h in range(PEER_HEADS):
        qh = jnp.dot(hn_s[...], wq_ref[:, 2 * h * PEER_HALF:2 * (h + 1) * PEER_HALF],
                     preferred_element_type=F32).astype(BF16)
        q_s[2 * h] = qh[:, :PEER_HALF]
        q_s[2 * h + 1] = qh[:, PEER_HALF:]

    def stage1(hp, carry):
        st = lax.dot_general(keys_ref[hp], q_s[hp], (((1,), (1,)), ((), ())),
                             preferred_element_type=F32)
        for c in range(SUBLANES):
            sc_s[pl.ds(c, N_KEYS, stride=SUBLANES), :] = st[:, c * LANES:(c + 1) * LANES]
        rv = ri = None
        for grp in range(N_KEYS // PEER_TOPK):
            v = [sc_s[_vreg_rows(grp * PEER_TOPK + i), :] for i in range(PEER_TOPK)]
            ix = [float(grp * PEER_TOPK + i) for i in range(PEER_TOPK)]
            _sort16(v, ix)
            if rv is None:
                rv, ri = v, ix
            else:
                _merge_top16(rv, ri, v, ix)
        for k in range(PEER_TOPK):
            topv_s[hp, _vreg_rows(k), :] = rv[k]
            topi_s[hp, _vreg_rows(k), :] = ri[k]
        return carry

    lax.fori_loop(0, 2 * PEER_HEADS, stage1, 0)

    n_wide = PEER_TOPK // 2
    cand_lists = [[(a, b) for b in range(PEER_TOPK // (a + 1))] for a in range(n_wide)]
    cand_lists.append([(a, 0) for a in range(n_wide, PEER_TOPK)])

    def stage2(h, carry):
        s1 = [topv_s[2 * h, _vreg_rows(k), :] for k in range(PEER_TOPK)]
        s2 = [topv_s[2 * h + 1, _vreg_rows(k), :] for k in range(PEER_TOPK)]
        rv = rf = None
        for n, pairs in enumerate(cand_lists):
            v = [s1[a] + s2[b] for a, b in pairs]
            flat = [float(a * PEER_TOPK + b) for a, b in pairs]
            if rv is None:
                rv, rf = v, flat
            else:
                _merge_top16(rv, rf, v, flat, sort=n < len(cand_lists) - 1)
        top = s1[0] + s2[0]
        ex = [jnp.exp(v - top) for v in rv]
        denom = ex[0]
        for e in ex[1:]:
            denom = denom + e
        inv = 1.0 / denom
        k1 = [topi_s[2 * h, _vreg_rows(k), :] for k in range(PEER_TOPK)]
        k2 = [topi_s[2 * h + 1, _vreg_rows(k), :] for k in range(PEER_TOPK)]
        for k in range(PEER_TOPK):
            pos1 = jnp.floor(rf[k] * (1.0 / PEER_TOPK))
            pos2 = rf[k] - pos1 * PEER_TOPK
            e1 = jnp.zeros_like(top)
            e2 = jnp.zeros_like(top)
            for a in range(PEER_TOPK):
                e1 = jnp.where(pos1 == a, k1[a], e1)
                e2 = jnp.where(pos2 == a, k2[a], e2)
            rows = pl.ds(pl.multiple_of((h * PEER_TOPK + k) * SUBLANES, SUBLANES), SUBLANES)
            e1_s[rows, :] = e1
            e2_s[rows, :] = e2
            gt_s[rows, :] = ex[k] * inv
        return carry

    lax.fori_loop(0, PEER_HEADS, stage2, 0)

    for src, dst in ((e1_s, i1_ref), (e2_s, i2_ref), (gt_s, gate_ref)):
        for c in range(SUBLANES):
            dst[c * LANES:(c + 1) * LANES, :] = src[pl.ds(c, LANES, stride=SUBLANES), :].T


def _peer_route(x, mod4, layer, g, wq, keys, *, tt, n_ctx_tiles, tiles_per_lat_seq):
    t, d = x.shape
    nsel = PEER_HEADS * PEER_TOPK
    assert tt == ROUTE_TILE and nsel == LANES
    out = jax.ShapeDtypeStruct((t, nsel), F32)
    vreg_table = pltpu.VMEM((nsel * SUBLANES, LANES), F32)
    return pl.pallas_call(
        _peer_route_kernel,
        grid=(t // tt,),
        scratch_shapes=[
            pltpu.VMEM((tt, d), BF16),
            pltpu.VMEM((2 * PEER_HEADS, tt, PEER_HALF), BF16),
            pltpu.VMEM((N_KEYS * SUBLANES, LANES), F32),
            pltpu.VMEM((2 * PEER_HEADS, PEER_TOPK * SUBLANES, LANES), F32),
            pltpu.VMEM((2 * PEER_HEADS, PEER_TOPK * SUBLANES, LANES), F32),
            vreg_table, vreg_table, vreg_table,
        ],
        in_specs=[
            pl.BlockSpec((tt, d), lambda i: (i, 0)),
            pl.BlockSpec((None, None, 6, d), _mod_row_map(layer, n_ctx_tiles, tiles_per_lat_seq)),
            pl.BlockSpec((1, d), lambda i: (0, 0)),
            pl.BlockSpec(wq.shape, lambda i: (0, 0)),
            pl.BlockSpec(keys.shape, lambda i: (0, 0, 0)),
        ],
        out_specs=[pl.BlockSpec((tt, nsel), lambda i: (i, 0))] * 3,
        out_shape=[out, out, out],
        compiler_params=_cparams(("arbitrary",)),
        name="peer_route",
    )(x, mod4, g, wq, keys)


_HIGH_HALF = 0xFFFF0000


def _pack_bf16_pair(lo, hi):
    lo_bits = lax.bitcast_convert_type(lo.astype(BF16).astype(F32), jnp.uint32)
    hi_bits = lax.bitcast_convert_type(hi.astype(BF16).astype(F32), jnp.uint32)
    return (lo_bits >> 16) | (hi_bits & jnp.uint32(_HIGH_HALF))


def _unpack_bf16_pair(words, high):
    bits = (words & jnp.uint32(_HIGH_HALF)) if high else (words << 16)
    return lax.bitcast_convert_type(bits, F32)


def _gate_pitch(tt):
    return tt + SUBLANES


def _peer_dense_kernel(final, tt, keys_per_blk, n_ctx_tiles, x_ref, mod_ref, g_ref, i1_ref, i2_ref,
                       gate_ref, u_ref, v_ref, gf_ref, *refs):
    *out_refs, hn_s, w_s, acc_s = refs
    eb = pl.program_id(1)
    pitch = _gate_pitch(tt)
    half = N_KEYS // 2

    @pl.when(eb == 0)
    def _():
        mod = mod_ref[...]
        hn_s[...] = _norm_mod(x_ref[...], g_ref[...], mod[4:5], mod[3:4]).astype(BF16)
        acc_s[...] = jnp.zeros_like(acc_s)
        key2_ids = lax.broadcasted_iota(jnp.int32, (N_KEYS, N_KEYS), 0)
        r = key2_ids & (half - 1)
        key1_ids = (((r >> 2) << 3) + (r & 3) + jnp.where(key2_ids >= half, 4, 0)).astype(F32)
        key2_ids = key2_ids.astype(F32)

        def build(t, carry):
            i1 = i1_ref[pl.ds(t, 1), :]
            i2 = i2_ref[pl.ds(t, 1), :]
            gt = gate_ref[pl.ds(t, 1), :]
            p = jnp.where(i1 == key1_ids, gt, 0.0).astype(BF16)
            q = jnp.where(i2 == key2_ids, 1.0, 0.0).astype(BF16)
            w = lax.dot_general(p, q, (((1,), (1,)), ((), ())), preferred_element_type=F32)
            w_s[pl.ds(t, half, stride=pitch), :] = _pack_bf16_pair(w[:half], w[half:])
            return carry

        lax.fori_loop(0, tt, build, 0, unroll=32)

    act = lax.dot_general(hn_s[...], u_ref[...], (((1,), (1,)), ((), ())),
                          preferred_element_type=F32)
    hs = [None] * keys_per_blk
    for oct_ in range(keys_per_blk // SUBLANES):
        for la in range(SUBLANES // 2):
            slab = (eb * (keys_per_blk // SUBLANES) + oct_) * (SUBLANES // 2) + la
            packed = w_s[pl.ds(pl.multiple_of(slab * pitch, SUBLANES), tt), :]
            for hi in range(2):
                a = oct_ * SUBLANES + hi * (SUBLANES // 2) + la
                wa = _unpack_bf16_pair(packed, hi)
                xa = act[:, a * N_KEYS:(a + 1) * N_KEYS]
                gelu = 0.5 * xa * (1.0 + lax.erf(xa * math.sqrt(0.5)))
                hs[a] = (gelu * wa).astype(BF16)
    h = jnp.concatenate(hs, axis=1)
    acc_s[...] += jnp.dot(h, v_ref[...], preferred_element_type=F32)

    @pl.when(eb == pl.num_programs(1) - 1)
    def _():
        y = x_ref[...] + mod_ref[...][5:6] * acc_s[...]
        if not final:
            out_refs[0][...] = y
        else:
            ms = jnp.mean(y * y, axis=-1, keepdims=True)
            y = y * lax.rsqrt(ms + EPS) * gf_ref[...]
            is_ctx = pl.program_id(0) < n_ctx_tiles

            @pl.when(is_ctx)
            def _():
                out_refs[0][...] = y

            @pl.when(jnp.logical_not(is_ctx))
            def _():
                out_refs[1][...] = y


def _peer_dense(x, mod4, layer, g, i1, i2, gate, u, v, g_final, *, tt, keys_per_blk,
                n_ctx_tiles, tiles_per_lat_seq, final_norm):
    t, d = x.shape
    n_exp = u.shape[1]
    eblk = keys_per_blk * N_KEYS
    nsel = i1.shape[1]
    pitch = _gate_pitch(tt)
    mod_map = _mod_row_map(layer, n_ctx_tiles, tiles_per_lat_seq)
    if final_norm:
        t_ctx = n_ctx_tiles * tt
        out_specs = [pl.BlockSpec((tt, d), lambda i, e: (jnp.minimum(i, n_ctx_tiles - 1), 0)),
                     pl.BlockSpec((tt, d), lambda i, e: (jnp.maximum(i - n_ctx_tiles, 0), 0))]
        out_shape = [jax.ShapeDtypeStruct((t_ctx, d), F32), jax.ShapeDtypeStruct((t - t_ctx, d), F32)]
    else:
        out_specs = pl.BlockSpec((tt, d), lambda i, e: (i, 0))
        out_shape = jax.ShapeDtypeStruct((t, d), F32)
    return pl.pallas_call(
        functools.partial(_peer_dense_kernel, final_norm, tt, keys_per_blk, n_ctx_tiles),
        grid=(t // tt, n_exp // eblk),
        in_specs=[
            pl.BlockSpec((tt, d), lambda i, e: (i, 0), pipeline_mode=pl.Buffered(1)),
            pl.BlockSpec((None, None, 6, d), mod_map),
            pl.BlockSpec((1, d), lambda i, e: (0, 0)),
            pl.BlockSpec((tt, nsel), lambda i, e: (i, 0), pipeline_mode=pl.Buffered(1)),
            pl.BlockSpec((tt, nsel), lambda i, e: (i, 0), pipeline_mode=pl.Buffered(1)),
            pl.BlockSpec((tt, nsel), lambda i, e: (i, 0), pipeline_mode=pl.Buffered(1)),
            pl.BlockSpec((None, eblk, d), lambda i, e: (layer, e, 0)),
            pl.BlockSpec((None, eblk, d), lambda i, e: (layer, e, 0)),
            pl.BlockSpec((1, d), lambda i, e: (0, 0)),
        ],
        out_specs=out_specs,
        out_shape=out_shape,
        scratch_shapes=[
            pltpu.VMEM((tt, d), BF16),
            pltpu.VMEM((N_KEYS // 2 * pitch, N_KEYS), jnp.uint32),
            pltpu.VMEM((tt, d), F32),
        ],
        compiler_params=_cparams(("arbitrary", "arbitrary")),
        name="peer_dense",
    )(x, mod4, g, i1, i2, gate, u, v, g_final)


def _softplus(x):
    return jnp.maximum(x, 0.0) + jnp.log(1.0 + jnp.exp(-jnp.abs(x)))


def _ssd_in_kernel(n_ctx_tiles, ctx_rowlen, x_ref, mod_ref, g_ref, wz_ref, wx_ref, wdt_ref,
                   cw_ref, cb_ref, dtb_ref, z_ref, xbc_ref, dt_ref):
    i = pl.program_id(0)
    x = x_ref[...]
    mod = mod_ref[...]
    hn = _norm_mod(x, g_ref[...], mod[1:2], mod[0:1]).astype(BF16)
    z_ref[...] = jnp.dot(hn, wz_ref[...], preferred_element_type=F32)
    rowlen = jnp.where(i < n_ctx_tiles, ctx_rowlen, GRID_W)
    pos = lax.broadcasted_iota(jnp.int32, (x.shape[0], 1), 0) & (rowlen - 1)
    chunk = 4 * LANES
    for c0 in range(0, xbc_ref.shape[1], chunk):
        cols = slice(c0, c0 + chunk)
        xbc = jnp.dot(hn, wx_ref[:, cols], preferred_element_type=F32)
        up, dn = _shift_rows(xbc, pos, rowlen)
        cw = cw_ref[:, cols]
        xbc_ref[:, cols] = _silu(up * cw[0:1] + xbc * cw[1:2] + dn * cw[2:3] + cb_ref[:, cols])
    dt = jnp.dot(hn, wdt_ref[...], preferred_element_type=F32)
    dtb = dtb_ref[...]
    dt_ref[0] = _softplus(dt[:, :LANES] + dtb[0:1])
    dt_ref[1] = _softplus(dt[:, LANES:] + dtb[1:2])


def _ssd_in(x, mod4, layer, g, w_z, w_x, w_dt, conv_w, conv_b, dt_bias, *, tt, n_ctx_tiles,
            ctx_rowlen, tiles_per_lat_seq):
    t, d = x.shape
    dz, dx = w_z.shape[1], w_x.shape[1]
    const = lambda i: (0, 0)
    return pl.pallas_call(
        functools.partial(_ssd_in_kernel, n_ctx_tiles, ctx_rowlen),
        grid=(t // tt,),
        in_specs=[
            pl.BlockSpec((tt, d), lambda i: (i, 0)),
            pl.BlockSpec((None, None, 6, d), _mod_row_map(layer, n_ctx_tiles, tiles_per_lat_seq)),
            pl.BlockSpec((1, d), const),
            pl.BlockSpec((d, dz), const),
            pl.BlockSpec((d, dx), const),
            pl.BlockSpec((d, 2 * LANES), const),
            pl.BlockSpec((CONV_W, dx), const),
            pl.BlockSpec((1, dx), const),
            pl.BlockSpec((2, LANES), const),
        ],
        out_specs=[
            pl.BlockSpec((tt, dz), lambda i: (i, 0)),
            pl.BlockSpec((tt, dx), lambda i: (i, 0)),
            pl.BlockSpec((2, tt, LANES), lambda i: (0, i, 0)),
        ],
        out_shape=[
            jax.ShapeDtypeStruct((t, dz), F32),
            jax.ShapeDtypeStruct((t, dx), F32),
            jax.ShapeDtypeStruct((2, t, LANES), F32),
        ],
        compiler_params=_cparams(("arbitrary",)),
        name="ssd_in",
    )(x, mod4, g, w_z, w_x, w_dt, conv_w, conv_b, dt_bias)


def _scan_chunk(direction, step, n_chunks):
    return jnp.where(direction == 0, step, n_chunks - 1 - step)


def _ssd_scan_kernel(n_heads, ctx_chunks, ctx_nc, lat_nc, xbc_ref, dt_ref, a_ref, e_ref, h0_ref,
                     y_ref, st_ref, state_s):
    direction = pl.program_id(0)
    chunk = _scan_chunk(direction, pl.program_id(1), pl.num_programs(1))
    q = SSD_CHUNK
    p = SSD_HEAD_DIM
    d_inner = n_heads * p
    gn = SSD_GROUPS * SSD_STATE
    heads_per_group = n_heads // SSD_GROUPS

    is_ctx = chunk < ctx_chunks
    seq_nc = jnp.where(is_ctx, ctx_nc, lat_nc)
    local = jnp.where(is_ctx, lax.rem(chunk, ctx_nc), lax.rem(chunk - ctx_chunks, lat_nc))
    seq_first = local == jnp.where(direction == 0, 0, seq_nc - 1)
    seq_last = local == jnp.where(direction == 0, seq_nc - 1, 0)

    @pl.when(jnp.logical_and(seq_first, is_ctx))
    def _():
        state_s[...] = jnp.zeros_like(state_s)

    @pl.when(jnp.logical_and(seq_first, jnp.logical_not(is_ctx)))
    def _():
        for pair in range(n_heads // 2):
            state_s[pair] = h0_ref[pair].T

    row = lax.broadcasted_iota(jnp.int32, (q, q), 0)
    col = lax.broadcasted_iota(jnp.int32, (q, q), 1)
    fwd = direction == 0
    mask = (col - row) * jnp.where(fwd, 1, -1) <= 0
    tri = jnp.where(mask, 1.0, 0.0)

    dt = dt_ref[...]
    dta = dt * (-jnp.exp(a_ref[...]))
    cum = jnp.dot(tri, dta, preferred_element_type=F32, precision=lax.Precision.HIGHEST)
    cum_t = cum.T
    tot = jnp.where(fwd, cum[q - 1:q, :], cum[0:1, :])

    def per_lane(m):
        e = e_ref[...]
        hi = m.astype(BF16)
        lo = (m - hi.astype(F32)).astype(BF16)
        return jnp.dot(hi, e, preferred_element_type=F32) + jnp.dot(lo, e, preferred_element_type=F32)

    exp_cum = jnp.exp(cum)
    dt_l = per_lane(dt)
    dt_end_l = per_lane(dt * jnp.exp(tot - cum))
    exp_cum_l = per_lane(exp_cum)
    exp_tot_l = jnp.where(fwd, exp_cum_l[q - 1:q, :], exp_cum_l[0:1, :])

    first = lax.broadcasted_iota(jnp.int32, (q, 2 * p), 1) < p

    for g in range(SSD_GROUPS):
        bm_f = xbc_ref[:, d_inner + g * SSD_STATE:d_inner + (g + 1) * SSD_STATE]
        bm = bm_f.astype(BF16)
        bm_t = bm_f.T.astype(BF16)
        cm = xbc_ref[:, d_inner + gn + g * SSD_STATE:d_inner + gn + (g + 1) * SSD_STATE].astype(BF16)
        cb = lax.dot_general(cm, bm, (((1,), (1,)), ((), ())), preferred_element_type=F32)
        for pr in range(heads_per_group // 2):
            hd = g * heads_per_group + 2 * pr
            pair = hd // 2
            lanes = slice(hd * p, (hd + 2) * p)
            xs = xbc_ref[:, lanes]
            xdt = xs * dt_l[:, lanes]
            ms = []
            for k in range(2):
                seg = cum[:, hd + k:hd + k + 1] - cum_t[hd + k:hd + k + 1, :]
                ms.append((jnp.where(mask, jnp.exp(seg), 0.0) * cb).astype(BF16))
            rhs = jnp.concatenate([jnp.where(first, xdt, 0.0), jnp.where(first, 0.0, xdt)],
                                  axis=0).astype(BF16)
            y = jnp.dot(jnp.concatenate(ms, axis=1), rhs, preferred_element_type=F32)
            h_prev = state_s[pair]
            y_off = jnp.dot(cm, h_prev.astype(BF16), preferred_element_type=F32)
            y_ref[:, lanes] = y + y_off * exp_cum_l[:, lanes]
            xdtw = (xs * dt_end_l[:, lanes]).astype(BF16)
            upd = jnp.dot(bm_t, xdtw, preferred_element_type=F32)
            state_s[pair] = exp_tot_l[:, lanes] * h_prev + upd

    @pl.when(jnp.logical_and(seq_last, is_ctx))
    def _():
        for pair in range(n_heads // 2):
            st_ref[pair] = state_s[pair].T


def _ssd_scan(xbc, dt, a_log, h0, *, n_heads, n_ctx_seq, ctx_nc, lat_nc):
    t, dx = xbc.shape
    d_inner = n_heads * SSD_HEAD_DIM
    n_pairs = n_heads // 2
    q = SSD_CHUNK
    n_chunks = t // q
    ctx_chunks = n_ctx_seq * ctx_nc
    state_block = (None, None, n_pairs, 2 * SSD_HEAD_DIM, SSD_STATE)
    assert 2 * SSD_HEAD_DIM == SSD_STATE
    head_lanes = (jnp.arange(d_inner)[None, :] // SSD_HEAD_DIM == jnp.arange(LANES)[:, None]).astype(BF16)

    def chunk_of(d, c):
        return _scan_chunk(d, c, n_chunks)

    def h0_map(d, c):
        return (jnp.maximum(chunk_of(d, c) - ctx_chunks, 0) // lat_nc, d, 0, 0, 0)

    def st_map(d, c):
        return (jnp.minimum(chunk_of(d, c), ctx_chunks - 1) // ctx_nc, d, 0, 0, 0)

    return pl.pallas_call(
        functools.partial(_ssd_scan_kernel, n_heads, ctx_chunks, ctx_nc, lat_nc),
        grid=(2, n_chunks),
        in_specs=[
            pl.BlockSpec((q, dx), lambda d, c: (chunk_of(d, c), 0)),
            pl.BlockSpec((None, q, LANES), lambda d, c: (d, chunk_of(d, c), 0)),
            pl.BlockSpec((None, 1, LANES), lambda d, c: (d, 0, 0)),
            pl.BlockSpec((LANES, d_inner), lambda d, c: (0, 0)),
            pl.BlockSpec(state_block, h0_map),
        ],
        out_specs=[
            pl.BlockSpec((None, q, d_inner), lambda d, c: (d, chunk_of(d, c), 0)),
            pl.BlockSpec(state_block, st_map),
        ],
        out_shape=[
            jax.ShapeDtypeStruct((2, t, d_inner), F32),
            jax.ShapeDtypeStruct((n_ctx_seq, 2, n_pairs, 2 * SSD_HEAD_DIM, SSD_STATE), F32),
        ],
        scratch_shapes=[pltpu.VMEM((n_pairs, 2 * SSD_HEAD_DIM, SSD_STATE), F32)],
        compiler_params=_cparams(("arbitrary", "arbitrary")),
        name="ssd_scan",
    )(xbc, dt, a_log, head_lanes, h0)


def _ssd_out_kernel(x_ref, mod_ref, y_ref, xs_ref, z_ref, dskip_ref, ng_ref, wout_ref, o_ref):
    y = y_ref[0] + y_ref[1] + dskip_ref[...] * xs_ref[...]
    yz = y * _silu(z_ref[...])
    ms = jnp.mean(yz * yz, axis=-1, keepdims=True)
    yn = (yz * lax.rsqrt(ms + EPS) * ng_ref[...]).astype(BF16)
    mix = jnp.dot(yn, wout_ref[...], preferred_element_type=F32)
    o_ref[...] = x_ref[...] + mod_ref[...][2:3] * mix


def _ssd_out(x, mod4, layer, y, xbc, z, d_skip, norm_g, w_out, *, tt, n_ctx_tiles,
             tiles_per_lat_seq):
    t, d = x.shape
    di = z.shape[1]
    const = lambda i: (0, 0)
    return pl.pallas_call(
        _ssd_out_kernel,
        grid=(t // tt,),
        in_specs=[
            pl.BlockSpec((tt, d), lambda i: (i, 0)),
            pl.BlockSpec((None, None, 6, d), _mod_row_map(layer, n_ctx_tiles, tiles_per_lat_seq)),
            pl.BlockSpec((2, tt, di), lambda i: (0, i, 0)),
            pl.BlockSpec((tt, di), lambda i: (i, 0)),
            pl.BlockSpec((tt, di), lambda i: (i, 0)),
            pl.BlockSpec((1, di), const),
            pl.BlockSpec((1, di), const),
            pl.BlockSpec((di, d), const),
        ],
        out_specs=pl.BlockSpec((tt, d), lambda i: (i, 0)),
        out_shape=jax.ShapeDtypeStruct((t, d), F32),
        compiler_params=_cparams(("arbitrary",)),
        name="ssd_out",
    )(x, mod4, y, xbc, z, d_skip, norm_g, w_out)


def _pick_tile(candidates, *lengths):
    for tt in candidates:
        if all(n % tt == 0 for n in lengths):
            return tt
    raise ValueError(f"no tile in {candidates} divides {lengths}")


def kernel(x_prompt, x_sample, state_ssm, c, c_ctx, norm_mix_g, norm_ffn_g, norm_f_g, ada_w, ada_b, sc_w_in, sc_conv_w, sc_w_out, ssd_w_in, ssd_conv_w, ssd_conv_b, ssd_dt_bias, ssd_a_log, ssd_d, ssd_norm_g, ssd_w_out, peer_wq, peer_keys, peer_u, peer_v):
    n_ctx_seq, ctx_len, d = x_prompt.shape
    n_lat_seq, lat_len, _ = x_sample.shape
    depth = ada_w.shape[0]
    t_ctx = n_ctx_seq * ctx_len
    d_inner = ssd_norm_g.shape[1]
    n_heads = d_inner // SSD_HEAD_DIM
    conv_dim = ssd_conv_w.shape[2]
    assert ctx_len & (ctx_len - 1) == 0 and ctx_len % SSD_CHUNK == 0 and lat_len % SSD_CHUNK == 0
    assert n_heads % (2 * SSD_GROUPS) == 0 and n_heads <= LANES

    x = None
    n_cond = 1 + n_lat_seq
    cond_rows = -(-n_cond // SUBLANES) * SUBLANES
    cond = jnp.concatenate([c_ctx[None], c, jnp.zeros((cond_rows - n_cond, d), F32)], axis=0)
    mod4 = _adaln(cond, ada_w, ada_b).reshape(depth, cond_rows, 6, d)

    def tiling(tt):
        return dict(tt=tt, n_ctx_tiles=t_ctx // tt, tiles_per_lat_seq=lat_len // tt)

    tt_mix = _pick_tile((512, 256), t_ctx, lat_len)
    tt_route = _pick_tile((ROUTE_TILE,), t_ctx, lat_len)
    tt_dense = _pick_tile((512, 256), t_ctx, lat_len)
    assert tt_mix % ctx_len == 0 and tt_mix % GRID_W == 0

    u_all, v_all = peer_u.astype(BF16), peer_v.astype(BF16)
    states = []
    for i in range(depth):
        j = i // 2
        if i % 2 == 0:
            if x is None:
                x_ctx, x_lat = x_prompt.reshape(t_ctx, d), x_sample.reshape(n_lat_seq * lat_len, d)
            else:
                x_ctx, x_lat = x[:t_ctx], x[t_ctx:]
            x = _conv_mixer(x_ctx, x_lat, mod4, i, norm_mix_g[i][None], sc_w_in[j].astype(BF16), sc_conv_w[j],
                            sc_w_out[j].astype(BF16), ctx_rowlen=ctx_len, **tiling(tt_mix))
        else:
            w_in = ssd_w_in[j]
            w_z = w_in[:, :d_inner].astype(BF16)
            w_x = w_in[:, d_inner:d_inner + conv_dim].astype(BF16)
            w_dt_raw = w_in[:, d_inner + conv_dim:]
            pad = jnp.zeros((d, LANES - n_heads), F32)
            w_dt = jnp.concatenate([w_dt_raw[:, :n_heads], pad, w_dt_raw[:, n_heads:], pad], axis=1).astype(BF16)
            lane_pad = ((0, 0), (0, LANES - n_heads))
            dt_bias = jnp.pad(ssd_dt_bias[j], lane_pad)
            a_log = jnp.pad(ssd_a_log[j], lane_pad)[:, None, :]
            z, xbc, dt = _ssd_in(x, mod4, i, norm_mix_g[i][None], w_z, w_x, w_dt, ssd_conv_w[j],
                                 ssd_conv_b[j][None], dt_bias, ctx_rowlen=ctx_len, **tiling(tt_mix))
            n_pairs = n_heads // 2
            h0 = state_ssm[:, j].reshape(n_lat_seq, 2, n_pairs, 2 * SSD_HEAD_DIM, SSD_STATE)
            y, st = _ssd_scan(xbc, dt, a_log, h0, n_heads=n_heads, n_ctx_seq=n_ctx_seq,
                              ctx_nc=ctx_len // SSD_CHUNK, lat_nc=lat_len // SSD_CHUNK)
            states.append(st.reshape(n_ctx_seq, 2, n_heads, SSD_HEAD_DIM, SSD_STATE))
            d_skip = jnp.repeat(ssd_d[j][0] + ssd_d[j][1], SSD_HEAD_DIM)[None]
            x = _ssd_out(x, mod4, i, y, xbc, z, d_skip, ssd_norm_g[j][None],
                         ssd_w_out[j].astype(BF16), **tiling(tt_mix))
        keys = peer_keys[i].reshape(2 * PEER_HEADS, N_KEYS, PEER_HALF).astype(BF16)
        i1, i2, gate = _peer_route(x, mod4, i, norm_ffn_g[i][None], peer_wq[i].astype(BF16), keys,
                                   **tiling(tt_route))
        x = _peer_dense(x, mod4, i, norm_ffn_g[i][None], i1, i2, gate, u_all, v_all,
                        norm_f_g[None], keys_per_blk=16,
                        final_norm=(i == depth - 1), **tiling(tt_dense))

    y_ctx, y_lat = x
    return (y_ctx.reshape(n_ctx_seq, ctx_len, d), y_lat.reshape(n_lat_seq, lat_len, d),
            jnp.stack(states, axis=1))
```

```python
import functools
import math

import jax
import jax.numpy as jnp
from jax import lax
from jax.experimental import pallas as pl
from jax.experimental.pallas import tpu as pltpu

F32 = jnp.float32
BF16 = jnp.bfloat16
EPS = 1e-6

LANES = 128
SUBLANES = 8
VMEM_LIMIT_BYTES = 60 * 1024 * 1024

GRID_W = 64
CONV_W = 3
SSD_HEAD_DIM = 64
SSD_STATE = 128
SSD_GROUPS = 8
SSD_CHUNK = 128
N_KEYS = 128
PEER_HEADS = 8
PEER_TOPK = 16
PEER_HALF = 128

NEG_INF = float("-inf")


def _cparams(sem):
    return pltpu.CompilerParams(dimension_semantics=sem, vmem_limit_bytes=VMEM_LIMIT_BYTES)


def _norm_mod(x, g, scl, sh):
    ms = jnp.mean(x * x, axis=-1, keepdims=True)
    return x * lax.rsqrt(ms + EPS) * g * (1.0 + scl) + sh


def _silu(x):
    return x * (1.0 / (1.0 + jnp.exp(-x)))


def _mod_row_map(layer, n_ctx_tiles, tiles_per_lat_seq):
    def index_map(i, *_):
        row = jnp.where(i < n_ctx_tiles, 0, 1 + (i - n_ctx_tiles) // tiles_per_lat_seq)
        return (layer, row, 0, 0)
    return index_map


def _shift_rows(u, pos, rowlen):
    n = u.shape[0]
    up = pltpu.roll(u, 1, axis=0)
    dn = pltpu.roll(u, n - 1, axis=0)
    up = jnp.where(pos == 0, 0.0, up)
    dn = jnp.where(pos == rowlen - 1, 0.0, dn)
    return up, dn


def _adaln_kernel(c_ref, w_ref, b_ref, o_ref):
    s = _silu(c_ref[...])
    o_ref[...] = jnp.dot(s, w_ref[...], preferred_element_type=F32,
                         precision=lax.Precision.HIGHEST) + b_ref[...]


def _adaln(cond, ada_w, ada_b):
    depth, d, n = ada_w.shape
    rows = cond.shape[0]
    tn = 1536
    return pl.pallas_call(
        _adaln_kernel,
        grid=(depth, n // tn),
        in_specs=[
            pl.BlockSpec((rows, d), lambda l, j: (0, 0)),
            pl.BlockSpec((None, d, tn), lambda l, j: (l, 0, j)),
            pl.BlockSpec((None, 1, tn), lambda l, j: (l, 0, j)),
        ],
        out_specs=pl.BlockSpec((None, rows, tn), lambda l, j: (l, 0, j)),
        out_shape=jax.ShapeDtypeStruct((depth, rows, n), F32),
        compiler_params=_cparams(("arbitrary", "arbitrary")),
        name="adaln",
    )(cond, ada_w, ada_b.reshape(depth, 1, n))


def _conv_mixer_kernel(n_ctx_tiles, ctx_rowlen, xc_ref, xl_ref, mod_ref, g_ref, win_ref, cw_ref,
                       wout_ref, o_ref):
    i = pl.program_id(0)
    x = jnp.where(i < n_ctx_tiles, xc_ref[...], xl_ref[...])
    d = x.shape[1]
    mod = mod_ref[...]
    hn = _norm_mod(x, g_ref[...], mod[1:2], mod[0:1]).astype(BF16)
    p = jnp.dot(hn, win_ref[...], preferred_element_type=F32)
    bg, cg, xv = p[:, :d], p[:, d:2 * d], p[:, 2 * d:]
    u = cg * xv
    rowlen = jnp.where(i < n_ctx_tiles, ctx_rowlen, GRID_W)
    pos = lax.broadcasted_iota(jnp.int32, (x.shape[0], 1), 0) & (rowlen - 1)
    up, dn = _shift_rows(u, pos, rowlen)
    cw = cw_ref[...]
    y = up * cw[0:1] + u * cw[1:2] + dn * cw[2:3]
    mix = jnp.dot((bg * y).astype(BF16), wout_ref[...], preferred_element_type=F32)
    o_ref[...] = x + mod[2:3] * mix


def _conv_mixer(x_ctx, x_lat, mod4, layer, g, w_in, conv_w, w_out, *, tt, n_ctx_tiles, ctx_rowlen,
                tiles_per_lat_seq):
    d = x_ctx.shape[1]
    t = x_ctx.shape[0] + x_lat.shape[0]
    return pl.pallas_call(
        functools.partial(_conv_mixer_kernel, n_ctx_tiles, ctx_rowlen),
        grid=(t // tt,),
        in_specs=[
            pl.BlockSpec((tt, d), lambda i: (jnp.minimum(i, n_ctx_tiles - 1), 0)),
            pl.BlockSpec((tt, d), lambda i: (jnp.maximum(i - n_ctx_tiles, 0), 0)),
            pl.BlockSpec((None, None, 6, d), _mod_row_map(layer, n_ctx_tiles, tiles_per_lat_seq)),
            pl.BlockSpec((1, d), lambda i: (0, 0)),
            pl.BlockSpec((d, 3 * d), lambda i: (0, 0)),
            pl.BlockSpec((CONV_W, d), lambda i: (0, 0)),
            pl.BlockSpec((d, d), lambda i: (0, 0)),
        ],
        out_specs=pl.BlockSpec((tt, d), lambda i: (i, 0)),
        out_shape=jax.ShapeDtypeStruct((t, d), F32),
        compiler_params=_cparams(("arbitrary",)),
        name="conv_mixer",
    )(x_ctx, x_lat, mod4, g, w_in, conv_w, w_out)


ROUTE_TILE = SUBLANES * LANES


def _oddeven_merge_sort_pairs(n):
    pairs = []
    p = 1
    while p < n:
        k = p
        while k >= 1:
            for j in range(k % p, n - k, 2 * k):
                for i in range(min(k, n - j - k)):
                    if (i + j) // (2 * p) == (i + j + k) // (2 * p):
                        pairs.append((i + j, i + j + k))
            k //= 2
        p *= 2
    return pairs


_SORT16 = _oddeven_merge_sort_pairs(PEER_TOPK)


def _goes_first(a, ia, b, ib):
    return (a > b) | ((a == b) & (ia < ib))


def _compare_exchange(v, ix, i, j, ids_ordered=False):
    a, b, ia, ib = v[i], v[j], ix[i], ix[j]
    f = (a >= b) if ids_ordered else _goes_first(a, ia, b, ib)
    v[i], v[j] = jnp.where(f, a, b), jnp.where(f, b, a)
    ix[i], ix[j] = jnp.where(f, ia, ib), jnp.where(f, ib, ia)


def _sort16(v, ix):
    for n, (i, j) in enumerate(_SORT16):
        _compare_exchange(v, ix, i, j, ids_ordered=n < PEER_TOPK // 2)


def _bitonic_sort16(v, ix):
    d = PEER_TOPK // 2
    while d >= 1:
        for i in range(PEER_TOPK):
            if i & d == 0:
                _compare_exchange(v, ix, i, i + d)
        d //= 2


def _merge_top16(rv, ri, xv, xi, sort=True):
    for k in range(len(xv)):
        i = PEER_TOPK - 1 - k
        f = _goes_first(rv[i], ri[i], xv[k], xi[k])
        rv[i] = jnp.where(f, rv[i], xv[k])
        ri[i] = jnp.where(f, ri[i], xi[k])
    if sort:
        _bitonic_sort16(rv, ri)


def _vreg_rows(k):
    return pl.ds(k * SUBLANES, SUBLANES)


def _peer_route_kernel(x_ref, mod_ref, g_ref, wq_ref, keys_ref, i1_ref, i2_ref, gate_ref,
                       hn_s, q_s, sc_s, topv_s, topi_s, e1_s, e2_s, gt_s):
    mod = mod_ref[...]
    hn_s[...] = _norm_mod(x_ref[...], g_ref[...], mod[4:5], mod[3:4]).astype(BF16)
    for h in range(PEER_HEADS):
        qh = jnp.dot(hn_s[...], wq_ref[:, 2 * h * PEER_HALF:2 * (h + 1) * PEER_HALF],
                     preferred_element_type=F32).astype(BF16)
        q_s[2 * h] = qh[:, :PEER_HALF]
        q_s[2 * h + 1] = qh[:, PEER_HALF:]

    def stage1(hp, carry):
        st = lax.dot_general(keys_ref[hp], q_s[hp], (((1,), (1,)), ((), ())),
                             preferred_element_type=F32)
        for c in range(SUBLANES):
            sc_s[pl.ds(c, N_KEYS, stride=SUBLANES), :] = st[:, c * LANES:(c + 1) * LANES]
        rv = ri = None
        for grp in range(N_KEYS // PEER_TOPK):
            v = [sc_s[_vreg_rows(grp * PEER_TOPK + i), :] for i in range(PEER_TOPK)]
            ix = [float(grp * PEER_TOPK + i) for i in range(PEER_TOPK)]
            _sort16(v, ix)
            if rv is None:
                rv, ri = v, ix
            else:
                _merge_top16(rv, ri, v, ix)
        for k in range(PEER_TOPK):
            topv_s[hp, _vreg_rows(k), :] = rv[k]
            topi_s[hp, _vreg_rows(k), :] = ri[k]
        return carry

    lax.fori_loop(0, 2 * PEER_HEADS, stage1, 0)

    n_wide = PEER_TOPK // 2
    cand_lists = [[(a, b) for b in range(PEER_TOPK // (a + 1))] for a in range(n_wide)]
    cand_lists.append([(a, 0) for a in range(n_wide, PEER_TOPK)])

    def stage2(h, carry):
        s1 = [topv_s[2 * h, _vreg_rows(k), :] for k in range(PEER_TOPK)]
        s2 = [topv_s[2 * h + 1, _vreg_rows(k), :] for k in range(PEER_TOPK)]
        rv = rf = None
        for n, pairs in enumerate(cand_lists):
            v = [s1[a] + s2[b] for a, b in pairs]
            flat = [float(a * PEER_TOPK + b) for a, b in pairs]
            if rv is None:
                rv, rf = v, flat
            else:
                _merge_top16(rv, rf, v, flat, sort=n < len(cand_lists) - 1)
        top = s1[0] + s2[0]
        ex = [jnp.exp(v - top) for v in rv]
        denom = ex[0]
        for e in ex[1:]:
            denom = denom + e
        inv = 1.0 / denom
        k1 = [topi_s[2 * h, _vreg_rows(k), :] for k in range(PEER_TOPK)]
        k2 = [topi_s[2 * h + 1, _vreg_rows(k), :] for k in range(PEER_TOPK)]
        for k in range(PEER_TOPK):
            pos1 = jnp.floor(rf[k] * (1.0 / PEER_TOPK))
            pos2 = rf[k] - pos1 * PEER_TOPK
            e1 = jnp.zeros_like(top)
            e2 = jnp.zeros_like(top)
            for a in range(PEER_TOPK):
                e1 = jnp.where(pos1 == a, k1[a], e1)
                e2 = jnp.where(pos2 == a, k2[a], e2)
            rows = pl.ds(pl.multiple_of((h * PEER_TOPK + k) * SUBLANES, SUBLANES), SUBLANES)
            e1_s[rows, :] = e1
            e2_s[rows, :] = e2
            gt_s[rows, :] = ex[k] * inv
        return carry

    lax.fori_loop(0, PEER_HEADS, stage2, 0)

    for src, dst in ((e1_s, i1_ref), (e2_s, i2_ref), (gt_s, gate_ref)):
        for c in range(SUBLANES):
            dst[c * LANES:(c + 1) * LANES, :] = src[pl.ds(c, LANES, stride=SUBLANES), :].T


def _peer_route(x, mod4, layer, g, wq, keys, *, tt, n_ctx_tiles, tiles_per_lat_seq):
    t, d = x.shape
    nsel = PEER_HEADS * PEER_TOPK
    assert tt == ROUTE_TILE and nsel == LANES
    out = jax.ShapeDtypeStruct((t, nsel), F32)
    vreg_table = pltpu.VMEM((nsel * SUBLANES, LANES), F32)
    return pl.pallas_call(
        _peer_route_kernel,
        grid=(t // tt,),
        scratch_shapes=[
            pltpu.VMEM((tt, d), BF16),
            pltpu.VMEM((2 * PEER_HEADS, tt, PEER_HALF), BF16),
            pltpu.VMEM((N_KEYS * SUBLANES, LANES), F32),
            pltpu.VMEM((2 * PEER_HEADS, PEER_TOPK * SUBLANES, LANES), F32),
            pltpu.VMEM((2 * PEER_HEADS, PEER_TOPK * SUBLANES, LANES), F32),
            vreg_table, vreg_table, vreg_table,
        ],
        in_specs=[
            pl.BlockSpec((tt, d), lambda i: (i, 0)),
            pl.BlockSpec((None, None, 6, d), _mod_row_map(layer, n_ctx_tiles, tiles_per_lat_seq)),
            pl.BlockSpec((1, d), lambda i: (0, 0)),
            pl.BlockSpec(wq.shape, lambda i: (0, 0)),
            pl.BlockSpec(keys.shape, lambda i: (0, 0, 0)),
        ],
        out_specs=[pl.BlockSpec((tt, nsel), lambda i: (i, 0))] * 3,
        out_shape=[out, out, out],
        compiler_params=_cparams(("arbitrary",)),
        name="peer_route",
    )(x, mod4, g, wq, keys)


_HIGH_HALF = 0xFFFF0000


def _pack_bf16_pair(lo, hi):
    lo_bits = lax.bitcast_convert_type(lo.astype(BF16).astype(F32), jnp.uint32)
    hi_bits = lax.bitcast_convert_type(hi.astype(BF16).astype(F32), jnp.uint32)
    return (lo_bits >> 16) | (hi_bits & jnp.uint32(_HIGH_HALF))


def _unpack_bf16_pair(words, high):
    bits = (words & jnp.uint32(_HIGH_HALF)) if high else (words << 16)
    return lax.bitcast_convert_type(bits, F32)


def _gate_pitch(tt):
    return tt + SUBLANES


def _peer_dense_kernel(final, tt, keys_per_blk, n_ctx_tiles, x_ref, mod_ref, g_ref, i1_ref, i2_ref,
                       gate_ref, u_ref, v_ref, gf_ref, *refs):
    *out_refs, hn_s, w_s, acc_s = refs
    eb = pl.program_id(1)
    pitch = _gate_pitch(tt)
    half = N_KEYS // 2

    @pl.when(eb == 0)
    def _():
        mod = mod_ref[...]
        hn_s[...] = _norm_mod(x_ref[...], g_ref[...], mod[4:5], mod[3:4]).astype(BF16)
        acc_s[...] = jnp.zeros_like(acc_s)
        key2_ids = lax.broadcasted_iota(jnp.int32, (N_KEYS, N_KEYS), 0)
        r = key2_ids & (half - 1)
        key1_ids = (((r >> 2) << 3) + (r & 3) + jnp.where(key2_ids >= half, 4, 0)).astype(F32)
        key2_ids = key2_ids.astype(F32)

        def build(t, carry):
            i1 = i1_ref[pl.ds(t, 1), :]
            i2 = i2_ref[pl.ds(t, 1), :]
            gt = gate_ref[pl.ds(t, 1), :]
            p = jnp.where(i1 == key1_ids, gt, 0.0).astype(BF16)
            q = jnp.where(i2 == key2_ids, 1.0, 0.0).astype(BF16)
            w = lax.dot_general(p, q, (((1,), (1,)), ((), ())), preferred_element_type=F32)
            w_s[pl.ds(t, half, stride=pitch), :] = _pack_bf16_pair(w[:half], w[half:])
            return carry

        lax.fori_loop(0, tt, build, 0, unroll=64)

    act = lax.dot_general(hn_s[...], u_ref[...], (((1,), (1,)), ((), ())),
                          preferred_element_type=F32)
    hs = [None] * keys_per_blk
    for oct_ in range(keys_per_blk // SUBLANES):
        for la in range(SUBLANES // 2):
            slab = (eb * (keys_per_blk // SUBLANES) + oct_) * (SUBLANES // 2) + la
            packed = w_s[pl.ds(pl.multiple_of(slab * pitch, SUBLANES), tt), :]
            for hi in range(2):
                a = oct_ * SUBLANES + hi * (SUBLANES // 2) + la
                wa = _unpack_bf16_pair(packed, hi)
                xa = act[:, a * N_KEYS:(a + 1) * N_KEYS]
                gelu = 0.5 * xa * (1.0 + lax.erf(xa * math.sqrt(0.5)))
                hs[a] = (gelu * wa).astype(BF16)
    h = jnp.concatenate(hs, axis=1)
    acc_s[...] += jnp.dot(h, v_ref[...], preferred_element_type=F32)

    @pl.when(eb == pl.num_programs(1) - 1)
    def _():
        y = x_ref[...] + mod_ref[...][5:6] * acc_s[...]
        if not final:
            out_refs[0][...] = y
        else:
            ms = jnp.mean(y * y, axis=-1, keepdims=True)
            y = y * lax.rsqrt(ms + EPS) * gf_ref[...]
            is_ctx = pl.program_id(0) < n_ctx_tiles

            @pl.when(is_ctx)
            def _():
                out_refs[0][...] = y

            @pl.when(jnp.logical_not(is_ctx))
            def _():
                out_refs[1][...] = y


def _peer_dense(x, mod4, layer, g, i1, i2, gate, u, v, g_final, *, tt, keys_per_blk,
                n_ctx_tiles, tiles_per_lat_seq, final_norm):
    t, d = x.shape
    n_exp = u.shape[1]
    eblk = keys_per_blk * N_KEYS
    nsel = i1.shape[1]
    pitch = _gate_pitch(tt)
    mod_map = _mod_row_map(layer, n_ctx_tiles, tiles_per_lat_seq)
    if final_norm:
        t_ctx = n_ctx_tiles * tt
        out_specs = [pl.BlockSpec((tt, d), lambda i, e: (jnp.minimum(i, n_ctx_tiles - 1), 0)),
                     pl.BlockSpec((tt, d), lambda i, e: (jnp.maximum(i - n_ctx_tiles, 0), 0))]
        out_shape = [jax.ShapeDtypeStruct((t_ctx, d), F32), jax.ShapeDtypeStruct((t - t_ctx, d), F32)]
    else:
        out_specs = pl.BlockSpec((tt, d), lambda i, e: (i, 0))
        out_shape = jax.ShapeDtypeStruct((t, d), F32)
    return pl.pallas_call(
        functools.partial(_peer_dense_kernel, final_norm, tt, keys_per_blk, n_ctx_tiles),
        grid=(t // tt, n_exp // eblk),
        in_specs=[
            pl.BlockSpec((tt, d), lambda i, e: (i, 0), pipeline_mode=pl.Buffered(1)),
            pl.BlockSpec((None, None, 6, d), mod_map),
            pl.BlockSpec((1, d), lambda i, e: (0, 0)),
            pl.BlockSpec((tt, nsel), lambda i, e: (i, 0), pipeline_mode=pl.Buffered(1)),
            pl.BlockSpec((tt, nsel), lambda i, e: (i, 0), pipeline_mode=pl.Buffered(1)),
            pl.BlockSpec((tt, nsel), lambda i, e: (i, 0), pipeline_mode=pl.Buffered(1)),
            pl.BlockSpec((None, eblk, d), lambda i, e: (layer, e, 0)),
            pl.BlockSpec((None, eblk, d), lambda i, e: (layer, e, 0)),
            pl.BlockSpec((1, d), lambda i, e: (0, 0)),
        ],
        out_specs=out_specs,
        out_shape=out_shape,
        scratch_shapes=[
            pltpu.VMEM((tt, d), BF16),
            pltpu.VMEM((N_KEYS // 2 * pitch, N_KEYS), jnp.uint32),
            pltpu.VMEM((tt, d), F32),
        ],
        compiler_params=_cparams(("arbitrary", "arbitrary")),
        name="peer_dense",
    )(x, mod4, g, i1, i2, gate, u, v, g_final)


def _softplus(x):
    return jnp.maximum(x, 0.0) + jnp.log(1.0 + jnp.exp(-jnp.abs(x)))


def _ssd_in_kernel(n_ctx_tiles, ctx_rowlen, x_ref, mod_ref, g_ref, wz_ref, wx_ref, wdt_ref,
                   cw_ref, cb_ref, dtb_ref, z_ref, xbc_ref, dt_ref):
    i = pl.program_id(0)
    x = x_ref[...]
    mod = mod_ref[...]
    hn = _norm_mod(x, g_ref[...], mod[1:2], mod[0:1]).astype(BF16)
    z_ref[...] = jnp.dot(hn, wz_ref[...], preferred_element_type=F32)
    rowlen = jnp.where(i < n_ctx_tiles, ctx_rowlen, GRID_W)
    pos = lax.broadcasted_iota(jnp.int32, (x.shape[0], 1), 0) & (rowlen - 1)
    chunk = 4 * LANES
    for c0 in range(0, xbc_ref.shape[1], chunk):
        cols = slice(c0, c0 + chunk)
        xbc = jnp.dot(hn, wx_ref[:, cols], preferred_element_type=F32)
        up, dn = _shift_rows(xbc, pos, rowlen)
        cw = cw_ref[:, cols]
        xbc_ref[:, cols] = _silu(up * cw[0:1] + xbc * cw[1:2] + dn * cw[2:3] + cb_ref[:, cols])
    dt = jnp.dot(hn, wdt_ref[...], preferred_element_type=F32)
    dtb = dtb_ref[...]
    dt_ref[0] = _softplus(dt[:, :LANES] + dtb[0:1])
    dt_ref[1] = _softplus(dt[:, LANES:] + dtb[1:2])


def _ssd_in(x, mod4, layer, g, w_z, w_x, w_dt, conv_w, conv_b, dt_bias, *, tt, n_ctx_tiles,
            ctx_rowlen, tiles_per_lat_seq):
    t, d = x.shape
    dz, dx = w_z.shape[1], w_x.shape[1]
    const = lambda i: (0, 0)
    return pl.pallas_call(
        functools.partial(_ssd_in_kernel, n_ctx_tiles, ctx_rowlen),
        grid=(t // tt,),
        in_specs=[
            pl.BlockSpec((tt, d), lambda i: (i, 0)),
            pl.BlockSpec((None, None, 6, d), _mod_row_map(layer, n_ctx_tiles, tiles_per_lat_seq)),
            pl.BlockSpec((1, d), const),
            pl.BlockSpec((d, dz), const),
            pl.BlockSpec((d, dx), const),
            pl.BlockSpec((d, 2 * LANES), const),
            pl.BlockSpec((CONV_W, dx), const),
            pl.BlockSpec((1, dx), const),
            pl.BlockSpec((2, LANES), const),
        ],
        out_specs=[
            pl.BlockSpec((tt, dz), lambda i: (i, 0)),
            pl.BlockSpec((tt, dx), lambda i: (i, 0)),
            pl.BlockSpec((2, tt, LANES), lambda i: (0, i, 0)),
        ],
        out_shape=[
            jax.ShapeDtypeStruct((t, dz), F32),
            jax.ShapeDtypeStruct((t, dx), F32),
            jax.ShapeDtypeStruct((2, t, LANES), F32),
        ],
        compiler_params=_cparams(("arbitrary",)),
        name="ssd_in",
    )(x, mod4, g, w_z, w_x, w_dt, conv_w, conv_b, dt_bias)


def _ssd_scan_kernel(n_heads, ctx_chunks, ctx_nc, lat_nc, xbc_f, xbc_b, dt_f, dt_b, a_ref, e_ref,
                     h0_f, h0_b, y_f, y_b, st_f, st_b, state_f, state_b):
    step = pl.program_id(0)
    geometry = (n_heads, ctx_chunks, ctx_nc, lat_nc)
    dirs = ((True, step, xbc_f, dt_f, a_ref[0], e_ref, h0_f, y_f, st_f, state_f),
            (False, pl.num_programs(0) - 1 - step, xbc_b, dt_b, a_ref[1], e_ref, h0_b, y_b, st_b,
             state_b))
    for phase in ("init", "main", "emit"):
        for args in dirs:
            _scan_direction(phase, geometry, *args)


def _scan_direction(phase, geometry, fwd, chunk, xbc_ref, dt_ref, a_row, e_ref, h0_ref, y_ref,
                    st_ref, state_s):
    n_heads, ctx_chunks, ctx_nc, lat_nc = geometry
    q = SSD_CHUNK
    p = SSD_HEAD_DIM
    d_inner = n_heads * p
    gn = SSD_GROUPS * SSD_STATE
    heads_per_group = n_heads // SSD_GROUPS

    is_ctx = chunk < ctx_chunks
    seq_nc = jnp.where(is_ctx, ctx_nc, lat_nc)
    local = jnp.where(is_ctx, lax.rem(chunk, ctx_nc), lax.rem(chunk - ctx_chunks, lat_nc))
    seq_first = local == (0 if fwd else seq_nc - 1)
    seq_last = local == (seq_nc - 1 if fwd else 0)

    if phase == "init":
        @pl.when(jnp.logical_and(seq_first, is_ctx))
        def _():
            state_s[...] = jnp.zeros_like(state_s)

        @pl.when(jnp.logical_and(seq_first, jnp.logical_not(is_ctx)))
        def _():
            for pair in range(n_heads // 2):
                state_s[pair] = h0_ref[pair].T
        return

    if phase == "emit":
        @pl.when(jnp.logical_and(seq_last, is_ctx))
        def _():
            for pair in range(n_heads // 2):
                st_ref[pair] = state_s[pair].T
        return

    row = lax.broadcasted_iota(jnp.int32, (q, q), 0)
    col = lax.broadcasted_iota(jnp.int32, (q, q), 1)
    mask = (col <= row) if fwd else (col >= row)
    tri = jnp.where(mask, 1.0, 0.0)

    dt = dt_ref[...]
    dta = dt * (-jnp.exp(a_row))
    cum = jnp.dot(tri, dta, preferred_element_type=F32, precision=lax.Precision.HIGHEST)
    cum_t = cum.T
    tot = cum[q - 1:q, :] if fwd else cum[0:1, :]

    def per_lane(m):
        e = e_ref[...]
        hi = m.astype(BF16)
        lo = (m - hi.astype(F32)).astype(BF16)
        return jnp.dot(hi, e, preferred_element_type=F32) + jnp.dot(lo, e, preferred_element_type=F32)

    exp_cum = jnp.exp(cum)
    dt_l = per_lane(dt)
    dt_end_l = per_lane(dt * jnp.exp(tot - cum))
    exp_cum_l = per_lane(exp_cum)
    exp_tot_l = exp_cum_l[q - 1:q, :] if fwd else exp_cum_l[0:1, :]

    first = lax.broadcasted_iota(jnp.int32, (q, 2 * p), 1) < p

    for g in range(SSD_GROUPS):
        bm_f = xbc_ref[:, d_inner + g * SSD_STATE:d_inner + (g + 1) * SSD_STATE]
        bm = bm_f.astype(BF16)
        bm_t = bm_f.T.astype(BF16)
        cm = xbc_ref[:, d_inner + gn + g * SSD_STATE:d_inner + gn + (g + 1) * SSD_STATE].astype(BF16)
        cb = lax.dot_general(cm, bm, (((1,), (1,)), ((), ())), preferred_element_type=F32)
        for pr in range(heads_per_group // 2):
            hd = g * heads_per_group + 2 * pr
            pair = hd // 2
            lanes = slice(hd * p, (hd + 2) * p)
            xs = xbc_ref[:, lanes]
            xdt = xs * dt_l[:, lanes]
            ms = []
            for k in range(2):
                seg = cum[:, hd + k:hd + k + 1] - cum_t[hd + k:hd + k + 1, :]
                ms.append((jnp.where(mask, jnp.exp(seg), 0.0) * cb).astype(BF16))
            rhs = jnp.concatenate([jnp.where(first, xdt, 0.0), jnp.where(first, 0.0, xdt)],
                                  axis=0).astype(BF16)
            y = jnp.dot(jnp.concatenate(ms, axis=1), rhs, preferred_element_type=F32)
            h_prev = state_s[pair]
            y_off = jnp.dot(cm, h_prev.astype(BF16), preferred_element_type=F32)
            y_ref[:, lanes] = y + y_off * exp_cum_l[:, lanes]
            xdtw = (xs * dt_end_l[:, lanes]).astype(BF16)
            upd = jnp.dot(bm_t, xdtw, preferred_element_type=F32)
            state_s[pair] = exp_tot_l[:, lanes] * h_prev + upd


def _ssd_scan(xbc, dt, a_log, h0, *, n_heads, n_ctx_seq, ctx_nc, lat_nc):
    t, dx = xbc.shape
    d_inner = n_heads * SSD_HEAD_DIM
    n_pairs = n_heads // 2
    q = SSD_CHUNK
    n_chunks = t // q
    ctx_chunks = n_ctx_seq * ctx_nc
    state_tile = (n_pairs, 2 * SSD_HEAD_DIM, SSD_STATE)
    assert 2 * SSD_HEAD_DIM == SSD_STATE
    head_lanes = (jnp.arange(d_inner)[None, :] // SSD_HEAD_DIM == jnp.arange(LANES)[:, None]).astype(BF16)

    fwd_chunk = lambda s: s
    bwd_chunk = lambda s: n_chunks - 1 - s

    def h0_seq(chunk):
        return jnp.maximum(chunk - ctx_chunks, 0) // lat_nc

    def st_seq(chunk):
        return jnp.minimum(chunk, ctx_chunks - 1) // ctx_nc

    y_shape = jax.ShapeDtypeStruct((t, d_inner), F32)
    st_shape = jax.ShapeDtypeStruct((n_ctx_seq,) + state_tile, F32)
    y_f, y_b, st_f, st_b = pl.pallas_call(
        functools.partial(_ssd_scan_kernel, n_heads, ctx_chunks, ctx_nc, lat_nc),
        grid=(n_chunks,),
        in_specs=[
            pl.BlockSpec((q, dx), lambda s: (fwd_chunk(s), 0)),
            pl.BlockSpec((q, dx), lambda s: (bwd_chunk(s), 0)),
            pl.BlockSpec((None, q, LANES), lambda s: (0, fwd_chunk(s), 0)),
            pl.BlockSpec((None, q, LANES), lambda s: (1, bwd_chunk(s), 0)),
            pl.BlockSpec((2, 1, LANES), lambda s: (0, 0, 0)),
            pl.BlockSpec((LANES, d_inner), lambda s: (0, 0)),
            pl.BlockSpec((None, None) + state_tile, lambda s: (h0_seq(fwd_chunk(s)), 0, 0, 0, 0)),
            pl.BlockSpec((None, None) + state_tile, lambda s: (h0_seq(bwd_chunk(s)), 1, 0, 0, 0)),
        ],
        out_specs=[
            pl.BlockSpec((q, d_inner), lambda s: (fwd_chunk(s), 0)),
            pl.BlockSpec((q, d_inner), lambda s: (bwd_chunk(s), 0)),
            pl.BlockSpec((None,) + state_tile, lambda s: (st_seq(fwd_chunk(s)), 0, 0, 0)),
            pl.BlockSpec((None,) + state_tile, lambda s: (st_seq(bwd_chunk(s)), 0, 0, 0)),
        ],
        out_shape=[y_shape, y_shape, st_shape, st_shape],
        scratch_shapes=[pltpu.VMEM(state_tile, F32), pltpu.VMEM(state_tile, F32)],
        compiler_params=_cparams(("arbitrary",)),
        name="ssd_scan",
    )(xbc, xbc, dt, dt, a_log, head_lanes, h0, h0)
    return y_f, y_b, jnp.stack([st_f, st_b], axis=1)


def _ssd_out_kernel(x_ref, mod_ref, yf_ref, yb_ref, xs_ref, z_ref, dskip_ref, ng_ref, wout_ref,
                    o_ref):
    y = yf_ref[...] + yb_ref[...] + dskip_ref[...] * xs_ref[...]
    yz = y * _silu(z_ref[...])
    ms = jnp.mean(yz * yz, axis=-1, keepdims=True)
    yn = (yz * lax.rsqrt(ms + EPS) * ng_ref[...]).astype(BF16)
    mix = jnp.dot(yn, wout_ref[...], preferred_element_type=F32)
    o_ref[...] = x_ref[...] + mod_ref[...][2:3] * mix


def _ssd_out(x, mod4, layer, y_f, y_b, xbc, z, d_skip, norm_g, w_out, *, tt, n_ctx_tiles,
             tiles_per_lat_seq):
    t, d = x.shape
    di = z.shape[1]
    const = lambda i: (0, 0)
    return pl.pallas_call(
        _ssd_out_kernel,
        grid=(t // tt,),
        in_specs=[
            pl.BlockSpec((tt, d), lambda i: (i, 0)),
            pl.BlockSpec((None, None, 6, d), _mod_row_map(layer, n_ctx_tiles, tiles_per_lat_seq)),
            pl.BlockSpec((tt, di), lambda i: (i, 0)),
            pl.BlockSpec((tt, di), lambda i: (i, 0)),
            pl.BlockSpec((tt, di), lambda i: (i, 0)),
            pl.BlockSpec((tt, di), lambda i: (i, 0)),
            pl.BlockSpec((1, di), const),
            pl.BlockSpec((1, di), const),
            pl.BlockSpec((di, d), const),
        ],
        out_specs=pl.BlockSpec((tt, d), lambda i: (i, 0)),
        out_shape=jax.ShapeDtypeStruct((t, d), F32),
        compiler_params=_cparams(("arbitrary",)),
        name="ssd_out",
    )(x, mod4, y_f, y_b, xbc, z, d_skip, norm_g, w_out)


def _pick_tile(candidates, *lengths):
    for tt in candidates:
        if all(n % tt == 0 for n in lengths):
            return tt
    raise ValueError(f"no tile in {candidates} divides {lengths}")


def kernel(x_prompt, x_sample, state_ssm, c, c_ctx, norm_mix_g, norm_ffn_g, norm_f_g, ada_w, ada_b, sc_w_in, sc_conv_w, sc_w_out, ssd_w_in, ssd_conv_w, ssd_conv_b, ssd_dt_bias, ssd_a_log, ssd_d, ssd_norm_g, ssd_w_out, peer_wq, peer_keys, peer_u, peer_v):
    n_ctx_seq, ctx_len, d = x_prompt.shape
    n_lat_seq, lat_len, _ = x_sample.shape
    depth = ada_w.shape[0]
    t_ctx = n_ctx_seq * ctx_len
    d_inner = ssd_norm_g.shape[1]
    n_heads = d_inner // SSD_HEAD_DIM
    conv_dim = ssd_conv_w.shape[2]
    assert ctx_len & (ctx_len - 1) == 0 and ctx_len % SSD_CHUNK == 0 and lat_len % SSD_CHUNK == 0
    assert n_heads % (2 * SSD_GROUPS) == 0 and n_heads <= LANES

    x = None
    n_cond = 1 + n_lat_seq
    cond_rows = -(-n_cond // SUBLANES) * SUBLANES
    cond = jnp.concatenate([c_ctx[None], c, jnp.zeros((cond_rows - n_cond, d), F32)], axis=0)
    mod4 = _adaln(cond, ada_w, ada_b).reshape(depth, cond_rows, 6, d)

    def tiling(tt):
        return dict(tt=tt, n_ctx_tiles=t_ctx // tt, tiles_per_lat_seq=lat_len // tt)

    tt_mix = _pick_tile((512, 256), t_ctx, lat_len)
    tt_route = _pick_tile((ROUTE_TILE,), t_ctx, lat_len)
    tt_dense = _pick_tile((512, 256), t_ctx, lat_len)
    assert tt_mix % ctx_len == 0 and tt_mix % GRID_W == 0

    u_all, v_all = peer_u.astype(BF16), peer_v.astype(BF16)
    states = []
    for i in range(depth):
        j = i // 2
        if i % 2 == 0:
            if x is None:
                x_ctx, x_lat = x_prompt.reshape(t_ctx, d), x_sample.reshape(n_lat_seq * lat_len, d)
            else:
                x_ctx, x_lat = x[:t_ctx], x[t_ctx:]
            x = _conv_mixer(x_ctx, x_lat, mod4, i, norm_mix_g[i][None], sc_w_in[j].astype(BF16), sc_conv_w[j],
                            sc_w_out[j].astype(BF16), ctx_rowlen=ctx_len, **tiling(tt_mix))
        else:
            w_in = ssd_w_in[j]
            w_z = w_in[:, :d_inner].astype(BF16)
            w_x = w_in[:, d_inner:d_inner + conv_dim].astype(BF16)
            w_dt_raw = w_in[:, d_inner + conv_dim:]
            pad = jnp.zeros((d, LANES - n_heads), F32)
            w_dt = jnp.concatenate([w_dt_raw[:, :n_heads], pad, w_dt_raw[:, n_heads:], pad], axis=1).astype(BF16)
            lane_pad = ((0, 0), (0, LANES - n_heads))
            dt_bias = jnp.pad(ssd_dt_bias[j], lane_pad)
            a_log = jnp.pad(ssd_a_log[j], lane_pad)[:, None, :]
            z, xbc, dt = _ssd_in(x, mod4, i, norm_mix_g[i][None], w_z, w_x, w_dt, ssd_conv_w[j],
                                 ssd_conv_b[j][None], dt_bias, ctx_rowlen=ctx_len, **tiling(tt_mix))
            n_pairs = n_heads // 2
            h0 = state_ssm[:, j].reshape(n_lat_seq, 2, n_pairs, 2 * SSD_HEAD_DIM, SSD_STATE)
            y_f, y_b, st = _ssd_scan(xbc, dt, a_log, h0, n_heads=n_heads, n_ctx_seq=n_ctx_seq,
                                     ctx_nc=ctx_len // SSD_CHUNK, lat_nc=lat_len // SSD_CHUNK)
            states.append(st.reshape(n_ctx_seq, 2, n_heads, SSD_HEAD_DIM, SSD_STATE))
            d_skip = jnp.repeat(ssd_d[j][0] + ssd_d[j][1], SSD_HEAD_DIM)[None]
            x = _ssd_out(x, mod4, i, y_f, y_b, xbc, z, d_skip, ssd_norm_g[j][None],
                         ssd_w_out[j].astype(BF16), **tiling(tt_mix))
        keys = peer_keys[i].reshape(2 * PEER_HEADS, N_KEYS, PEER_HALF).astype(BF16)
        i1, i2, gate = _peer_route(x, mod4, i, norm_ffn_g[i][None], peer_wq[i].astype(BF16), keys,
                                   **tiling(tt_route))
        x = _peer_dense(x, mod4, i, norm_ffn_g[i][None], i1, i2, gate, u_all, v_all,
                        norm_f_g[None], keys_per_blk=16,
                        final_norm=(i == depth - 1), **tiling(tt_dense))

    y_ctx, y_lat = x
    return (y_ctx.reshape(n_ctx_seq, ctx_len, d), y_lat.reshape(n_lat_seq, lat_len, d),
            jnp.stack(states, axis=1))
```

```python
import functools
import math

import jax
import jax.numpy as jnp
from jax import lax
from jax.experimental import pallas as pl
from jax.experimental.pallas import tpu as pltpu

F32 = jnp.float32
BF16 = jnp.bfloat16
EPS = 1e-6

LANES = 128
SUBLANES = 8
VMEM_LIMIT_BYTES = 60 * 1024 * 1024

GRID_W = 64
CONV_W = 3
SSD_HEAD_DIM = 64
SSD_STATE = 128
SSD_GROUPS = 8
SSD_CHUNK = 128
N_KEYS = 128
PEER_HEADS = 8
PEER_TOPK = 16
PEER_HALF = 128

NEG_INF = float("-inf")


def _cparams(sem):
    return pltpu.CompilerParams(dimension_semantics=sem, vmem_limit_bytes=VMEM_LIMIT_BYTES)


def _norm_mod(x, g, scl, sh):
    ms = jnp.mean(x * x, axis=-1, keepdims=True)
    return x * lax.rsqrt(ms + EPS) * g * (1.0 + scl) + sh


def _silu(x):
    return x * (1.0 / (1.0 + jnp.exp(-x)))


def _mod_row_map(layer, n_ctx_tiles, tiles_per_lat_seq):
    def index_map(i, *_):
        row = jnp.where(i < n_ctx_tiles, 0, 1 + (i - n_ctx_tiles) // tiles_per_lat_seq)
        return (layer, row, 0, 0)
    return index_map


def _shift_rows(u, pos, rowlen):
    n = u.shape[0]
    up = pltpu.roll(u, 1, axis=0)
    dn = pltpu.roll(u, n - 1, axis=0)
    up = jnp.where(pos == 0, 0.0, up)
    dn = jnp.where(pos == rowlen - 1, 0.0, dn)
    return up, dn


def _adaln_kernel(c_ref, w_ref, b_ref, o_ref):
    s = _silu(c_ref[...])
    o_ref[...] = jnp.dot(s, w_ref[...], preferred_element_type=F32,
                         precision=lax.Precision.HIGHEST) + b_ref[...]


def _adaln(cond, ada_w, ada_b):
    depth, d, n = ada_w.shape
    rows = cond.shape[0]
    tn = 1536
    return pl.pallas_call(
        _adaln_kernel,
        grid=(depth, n // tn),
        in_specs=[
            pl.BlockSpec((rows, d), lambda l, j: (0, 0)),
            pl.BlockSpec((None, d, tn), lambda l, j: (l, 0, j)),
            pl.BlockSpec((None, 1, tn), lambda l, j: (l, 0, j)),
        ],
        out_specs=pl.BlockSpec((None, rows, tn), lambda l, j: (l, 0, j)),
        out_shape=jax.ShapeDtypeStruct((depth, rows, n), F32),
        compiler_params=_cparams(("arbitrary", "arbitrary")),
        name="adaln",
    )(cond, ada_w, ada_b.reshape(depth, 1, n))


def _conv_mixer_kernel(n_ctx_tiles, ctx_rowlen, xc_ref, xl_ref, mod_ref, g_ref, win_ref, cw_ref,
                       wout_ref, o_ref):
    i = pl.program_id(0)
    x = jnp.where(i < n_ctx_tiles, xc_ref[...], xl_ref[...])
    d = x.shape[1]
    mod = mod_ref[...]
    hn = _norm_mod(x, g_ref[...], mod[1:2], mod[0:1]).astype(BF16)
    p = jnp.dot(hn, win_ref[...], preferred_element_type=F32)
    bg, cg, xv = p[:, :d], p[:, d:2 * d], p[:, 2 * d:]
    u = cg * xv
    rowlen = jnp.where(i < n_ctx_tiles, ctx_rowlen, GRID_W)
    pos = lax.broadcasted_iota(jnp.int32, (x.shape[0], 1), 0) & (rowlen - 1)
    up, dn = _shift_rows(u, pos, rowlen)
    cw = cw_ref[...]
    y = up * cw[0:1] + u * cw[1:2] + dn * cw[2:3]
    mix = jnp.dot((bg * y).astype(BF16), wout_ref[...], preferred_element_type=F32)
    o_ref[...] = x + mod[2:3] * mix


def _conv_mixer(x_ctx, x_lat, mod4, layer, g, w_in, conv_w, w_out, *, tt, n_ctx_tiles, ctx_rowlen,
                tiles_per_lat_seq):
    d = x_ctx.shape[1]
    t = x_ctx.shape[0] + x_lat.shape[0]
    return pl.pallas_call(
        functools.partial(_conv_mixer_kernel, n_ctx_tiles, ctx_rowlen),
        grid=(t // tt,),
        in_specs=[
            pl.BlockSpec((tt, d), lambda i: (jnp.minimum(i, n_ctx_tiles - 1), 0)),
            pl.BlockSpec((tt, d), lambda i: (jnp.maximum(i - n_ctx_tiles, 0), 0)),
            pl.BlockSpec((None, None, 6, d), _mod_row_map(layer, n_ctx_tiles, tiles_per_lat_seq)),
            pl.BlockSpec((1, d), lambda i: (0, 0)),
            pl.BlockSpec((d, 3 * d), lambda i: (0, 0)),
            pl.BlockSpec((CONV_W, d), lambda i: (0, 0)),
            pl.BlockSpec((d, d), lambda i: (0, 0)),
        ],
        out_specs=pl.BlockSpec((tt, d), lambda i: (i, 0)),
        out_shape=jax.ShapeDtypeStruct((t, d), F32),
        compiler_params=_cparams(("arbitrary",)),
        name="conv_mixer",
    )(x_ctx, x_lat, mod4, g, w_in, conv_w, w_out)


ROUTE_TILE = SUBLANES * LANES


def _oddeven_merge_sort_pairs(n):
    pairs = []
    p = 1
    while p < n:
        k = p
        while k >= 1:
            for j in range(k % p, n - k, 2 * k):
                for i in range(min(k, n - j - k)):
                    if (i + j) // (2 * p) == (i + j + k) // (2 * p):
                        pairs.append((i + j, i + j + k))
            k //= 2
        p *= 2
    return pairs


_SORT16 = _oddeven_merge_sort_pairs(PEER_TOPK)


def _goes_first(a, ia, b, ib):
    return (a > b) | ((a == b) & (ia < ib))


def _compare_exchange(v, ix, i, j, ids_ordered=False):
    a, b, ia, ib = v[i], v[j], ix[i], ix[j]
    f = (a >= b) if ids_ordered else _goes_first(a, ia, b, ib)
    v[i], v[j] = jnp.where(f, a, b), jnp.where(f, b, a)
    ix[i], ix[j] = jnp.where(f, ia, ib), jnp.where(f, ib, ia)


def _sort16(v, ix):
    for n, (i, j) in enumerate(_SORT16):
        _compare_exchange(v, ix, i, j, ids_ordered=n < PEER_TOPK // 2)


def _bitonic_sort16(v, ix):
    d = PEER_TOPK // 2
    while d >= 1:
        for i in range(PEER_TOPK):
            if i & d == 0:
                _compare_exchange(v, ix, i, i + d)
        d //= 2


def _merge_top16(rv, ri, xv, xi, sort=True):
    for k in range(len(xv)):
        i = PEER_TOPK - 1 - k
        f = _goes_first(rv[i], ri[i], xv[k], xi[k])
        rv[i] = jnp.where(f, rv[i], xv[k])
        ri[i] = jnp.where(f, ri[i], xi[k])
    if sort:
        _bitonic_sort16(rv, ri)


def _vreg_rows(k):
    return pl.ds(k * SUBLANES, SUBLANES)


def _peer_route_kernel(x_ref, mod_ref, g_ref, wq_ref, keys_ref, i1_ref, i2_ref, gate_ref,
                       hn_s, q_s, sc_s, topv_s, topi_s, e1_s, e2_s, gt_s):
    mod = mod_ref[...]
    hn_s[...] = _norm_mod(x_ref[...], g_ref[...], mod[4:5], mod[3:4]).astype(BF16)
    for h in range(PEER_HEADS):
        qh = jnp.dot(hn_s[...], wq_ref[:, 2 * h * PEER_HALF:2 * (h + 1) * PEER_HALF],
                     preferred_element_type=F32).astype(BF16)
        q_s[2 * h] = qh[:, :PEER_HALF]
        q_s[2 * h + 1] = qh[:, PEER_HALF:]

    def stage1(hp, carry):
        st = lax.dot_general(keys_ref[hp], q_s[hp], (((1,), (1,)), ((), ())),
                             preferred_element_type=F32)
        for c in range(SUBLANES):
            sc_s[pl.ds(c, N_KEYS, stride=SUBLANES), :] = st[:, c * LANES:(c + 1) * LANES]
        rv = ri = None
        for grp in range(N_KEYS // PEER_TOPK):
            v = [sc_s[_vreg_rows(grp * PEER_TOPK + i), :] for i in range(PEER_TOPK)]
            ix = [float(grp * PEER_TOPK + i) for i in range(PEER_TOPK)]
            _sort16(v, ix)
            if rv is None:
                rv, ri = v, ix
            else:
                _merge_top16(rv, ri, v, ix)
        for k in range(PEER_TOPK):
            topv_s[hp, _vreg_rows(k), :] = rv[k]
            topi_s[hp, _vreg_rows(k), :] = ri[k]
        return carry

    lax.fori_loop(0, 2 * PEER_HEADS, stage1, 0)

    n_wide = PEER_TOPK // 2
    cand_lists = [[(a, b) for b in range(PEER_TOPK // (a + 1))] for a in range(n_wide)]
    cand_lists.append([(a, 0) for a in range(n_wide, PEER_TOPK)])

    def stage2(h, carry):
        s1 = [topv_s[2 * h, _vreg_rows(k), :] for k in range(PEER_TOPK)]
        s2 = [topv_s[2 * h + 1, _vreg_rows(k), :] for k in range(PEER_TOPK)]
        rv = rf = None
        for n, pairs in enumerate(cand_lists):
            v = [s1[a] + s2[b] for a, b in pairs]
            flat = [float(a * PEER_TOPK + b) for a, b in pairs]
            if rv is None:
                rv, rf = v, flat
            else:
                _merge_top16(rv, rf, v, flat, sort=n < len(cand_lists) - 1)
        top = s1[0] + s2[0]
        ex = [jnp.exp(v - top) for v in rv]
        denom = ex[0]
        for e in ex[1:]:
            denom = denom + e
        inv = 1.0 / denom
        k1 = [topi_s[2 * h, _vreg_rows(k), :] for k in range(PEER_TOPK)]
        k2 = [topi_s[2 * h + 1, _vreg_rows(k), :] for k in range(PEER_TOPK)]
        for k in range(PEER_TOPK):
            pos1 = jnp.floor(rf[k] * (1.0 / PEER_TOPK))
            pos2 = rf[k] - pos1 * PEER_TOPK
            e1 = jnp.zeros_like(top)
            e2 = jnp.zeros_like(top)
            for a in range(PEER_TOPK):
                e1 = jnp.where(pos1 == a, k1[a], e1)
                e2 = jnp.where(pos2 == a, k2[a], e2)
            rows = pl.ds(pl.multiple_of((h * PEER_TOPK + k) * SUBLANES, SUBLANES), SUBLANES)
            e1_s[rows, :] = e1
            e2_s[rows, :] = e2
            gt_s[rows, :] = ex[k] * inv
        return carry

    lax.fori_loop(0, PEER_HEADS, stage2, 0)

    for src, dst in ((e1_s, i1_ref), (e2_s, i2_ref), (gt_s, gate_ref)):
        for c in range(SUBLANES):
            dst[c * LANES:(c + 1) * LANES, :] = src[pl.ds(c, LANES, stride=SUBLANES), :].T


def _peer_route(x, mod4, layer, g, wq, keys, *, tt, n_ctx_tiles, tiles_per_lat_seq):
    t, d = x.shape
    nsel = PEER_HEADS * PEER_TOPK
    assert tt == ROUTE_TILE and nsel == LANES
    out = jax.ShapeDtypeStruct((t, nsel), F32)
    vreg_table = pltpu.VMEM((nsel * SUBLANES, LANES), F32)
    return pl.pallas_call(
        _peer_route_kernel,
        grid=(t // tt,),
        scratch_shapes=[
            pltpu.VMEM((tt, d), BF16),
            pltpu.VMEM((2 * PEER_HEADS, tt, PEER_HALF), BF16),
            pltpu.VMEM((N_KEYS * SUBLANES, LANES), F32),
            pltpu.VMEM((2 * PEER_HEADS, PEER_TOPK * SUBLANES, LANES), F32),
            pltpu.VMEM((2 * PEER_HEADS, PEER_TOPK * SUBLANES, LANES), F32),
            vreg_table, vreg_table, vreg_table,
        ],
        in_specs=[
            pl.BlockSpec((tt, d), lambda i: (i, 0)),
            pl.BlockSpec((None, None, 6, d), _mod_row_map(layer, n_ctx_tiles, tiles_per_lat_seq)),
            pl.BlockSpec((1, d), lambda i: (0, 0)),
            pl.BlockSpec(wq.shape, lambda i: (0, 0)),
            pl.BlockSpec(keys.shape, lambda i: (0, 0, 0)),
        ],
        out_specs=[pl.BlockSpec((tt, nsel), lambda i: (i, 0))] * 3,
        out_shape=[out, out, out],
        compiler_params=_cparams(("arbitrary",)),
        name="peer_route",
    )(x, mod4, g, wq, keys)


_HIGH_HALF = 0xFFFF0000


def _pack_bf16_pair(lo, hi):
    lo_bits = lax.bitcast_convert_type(lo.astype(BF16).astype(F32), jnp.uint32)
    hi_bits = lax.bitcast_convert_type(hi.astype(BF16).astype(F32), jnp.uint32)
    return (lo_bits >> 16) | (hi_bits & jnp.uint32(_HIGH_HALF))


def _unpack_bf16_pair(words, high):
    bits = (words & jnp.uint32(_HIGH_HALF)) if high else (words << 16)
    return lax.bitcast_convert_type(bits, F32)


def _gate_pitch(tt):
    return tt + SUBLANES


def _peer_dense_kernel(final, tt, keys_per_blk, n_ctx_tiles, x_ref, mod_ref, g_ref, i1_ref, i2_ref,
                       gate_ref, u_ref, v_ref, gf_ref, *refs):
    *out_refs, hn_s, w_s, acc_s = refs
    eb = pl.program_id(1)
    pitch = _gate_pitch(tt)
    half = N_KEYS // 2

    @pl.when(eb == 0)
    def _():
        mod = mod_ref[...]
        hn_s[...] = _norm_mod(x_ref[...], g_ref[...], mod[4:5], mod[3:4]).astype(BF16)
        acc_s[...] = jnp.zeros_like(acc_s)
        key2_ids = lax.broadcasted_iota(jnp.int32, (N_KEYS, N_KEYS), 0)
        r = key2_ids & (half - 1)
        key1_ids = (((r >> 2) << 3) + (r & 3) + jnp.where(key2_ids >= half, 4, 0)).astype(F32)
        key2_ids = key2_ids.astype(F32)

        def build(t, carry):
            i1 = i1_ref[pl.ds(t, 1), :]
            i2 = i2_ref[pl.ds(t, 1), :]
            gt = gate_ref[pl.ds(t, 1), :]
            p = jnp.where(i1 == key1_ids, gt, 0.0).astype(BF16)
            q = jnp.where(i2 == key2_ids, 1.0, 0.0).astype(BF16)
            w = lax.dot_general(p, q, (((1,), (1,)), ((), ())), preferred_element_type=F32)
            w_s[pl.ds(t, half, stride=pitch), :] = _pack_bf16_pair(w[:half], w[half:])
            return carry

        lax.fori_loop(0, tt, build, 0, unroll=64)

    act = jnp.dot(hn_s[...], u_ref[...], preferred_element_type=F32)
    hs = [None] * keys_per_blk
    for oct_ in range(keys_per_blk // SUBLANES):
        for la in range(SUBLANES // 2):
            slab = (eb * (keys_per_blk // SUBLANES) + oct_) * (SUBLANES // 2) + la
            packed = w_s[pl.ds(pl.multiple_of(slab * pitch, SUBLANES), tt), :]
            for hi in range(2):
                a = oct_ * SUBLANES + hi * (SUBLANES // 2) + la
                wa = _unpack_bf16_pair(packed, hi)
                xa = act[:, a * N_KEYS:(a + 1) * N_KEYS]
                gelu = 0.5 * xa * (1.0 + lax.erf(xa * math.sqrt(0.5)))
                hs[a] = (gelu * wa).astype(BF16)
    h = jnp.concatenate(hs, axis=1)
    acc_s[...] += jnp.dot(h, v_ref[...], preferred_element_type=F32)

    @pl.when(eb == pl.num_programs(1) - 1)
    def _():
        y = x_ref[...] + mod_ref[...][5:6] * acc_s[...]
        if not final:
            out_refs[0][...] = y
        else:
            ms = jnp.mean(y * y, axis=-1, keepdims=True)
            y = y * lax.rsqrt(ms + EPS) * gf_ref[...]
            is_ctx = pl.program_id(0) < n_ctx_tiles

            @pl.when(is_ctx)
            def _():
                out_refs[0][...] = y

            @pl.when(jnp.logical_not(is_ctx))
            def _():
                out_refs[1][...] = y


def _peer_dense(x, mod4, layer, g, i1, i2, gate, u, v, g_final, *, tt, keys_per_blk,
                n_ctx_tiles, tiles_per_lat_seq, final_norm):
    t, d = x.shape
    n_exp = v.shape[1]
    eblk = keys_per_blk * N_KEYS
    nsel = i1.shape[1]
    pitch = _gate_pitch(tt)
    mod_map = _mod_row_map(layer, n_ctx_tiles, tiles_per_lat_seq)
    if final_norm:
        t_ctx = n_ctx_tiles * tt
        out_specs = [pl.BlockSpec((tt, d), lambda i, e: (jnp.minimum(i, n_ctx_tiles - 1), 0)),
                     pl.BlockSpec((tt, d), lambda i, e: (jnp.maximum(i - n_ctx_tiles, 0), 0))]
        out_shape = [jax.ShapeDtypeStruct((t_ctx, d), F32), jax.ShapeDtypeStruct((t - t_ctx, d), F32)]
    else:
        out_specs = pl.BlockSpec((tt, d), lambda i, e: (i, 0))
        out_shape = jax.ShapeDtypeStruct((t, d), F32)
    return pl.pallas_call(
        functools.partial(_peer_dense_kernel, final_norm, tt, keys_per_blk, n_ctx_tiles),
        grid=(t // tt, n_exp // eblk),
        in_specs=[
            pl.BlockSpec((tt, d), lambda i, e: (i, 0), pipeline_mode=pl.Buffered(1)),
            pl.BlockSpec((None, None, 6, d), mod_map),
            pl.BlockSpec((1, d), lambda i, e: (0, 0)),
            pl.BlockSpec((tt, nsel), lambda i, e: (i, 0), pipeline_mode=pl.Buffered(1)),
            pl.BlockSpec((tt, nsel), lambda i, e: (i, 0), pipeline_mode=pl.Buffered(1)),
            pl.BlockSpec((tt, nsel), lambda i, e: (i, 0), pipeline_mode=pl.Buffered(1)),
            pl.BlockSpec((None, d, eblk), lambda i, e: (layer, 0, e)),
            pl.BlockSpec((None, eblk, d), lambda i, e: (layer, e, 0)),
            pl.BlockSpec((1, d), lambda i, e: (0, 0)),
        ],
        out_specs=out_specs,
        out_shape=out_shape,
        scratch_shapes=[
            pltpu.VMEM((tt, d), BF16),
            pltpu.VMEM((N_KEYS // 2 * pitch, N_KEYS), jnp.uint32),
            pltpu.VMEM((tt, d), F32),
        ],
        compiler_params=_cparams(("arbitrary", "arbitrary")),
        name="peer_dense",
    )(x, mod4, g, i1, i2, gate, u, v, g_final)


def _softplus(x):
    return jnp.maximum(x, 0.0) + jnp.log(1.0 + jnp.exp(-jnp.abs(x)))


def _ssd_in_kernel(n_ctx_tiles, ctx_rowlen, x_ref, mod_ref, g_ref, wz_ref, wx_ref, wdt_ref,
                   cw_ref, cb_ref, dtb_ref, z_ref, xbc_ref, dt_ref):
    i = pl.program_id(0)
    x = x_ref[...]
    mod = mod_ref[...]
    hn = _norm_mod(x, g_ref[...], mod[1:2], mod[0:1]).astype(BF16)
    z_ref[...] = jnp.dot(hn, wz_ref[...], preferred_element_type=F32)
    rowlen = jnp.where(i < n_ctx_tiles, ctx_rowlen, GRID_W)
    pos = lax.broadcasted_iota(jnp.int32, (x.shape[0], 1), 0) & (rowlen - 1)
    chunk = 4 * LANES
    for c0 in range(0, xbc_ref.shape[1], chunk):
        cols = slice(c0, c0 + chunk)
        xbc = jnp.dot(hn, wx_ref[:, cols], preferred_element_type=F32)
        up, dn = _shift_rows(xbc, pos, rowlen)
        cw = cw_ref[:, cols]
        xbc_ref[:, cols] = _silu(up * cw[0:1] + xbc * cw[1:2] + dn * cw[2:3] + cb_ref[:, cols])
    dt = jnp.dot(hn, wdt_ref[...], preferred_element_type=F32)
    dtb = dtb_ref[...]
    dt_ref[0] = _softplus(dt[:, :LANES] + dtb[0:1])
    dt_ref[1] = _softplus(dt[:, LANES:] + dtb[1:2])


def _ssd_in(x, mod4, layer, g, w_z, w_x, w_dt, conv_w, conv_b, dt_bias, *, tt, n_ctx_tiles,
            ctx_rowlen, tiles_per_lat_seq):
    t, d = x.shape
    dz, dx = w_z.shape[1], w_x.shape[1]
    const = lambda i: (0, 0)
    return pl.pallas_call(
        functools.partial(_ssd_in_kernel, n_ctx_tiles, ctx_rowlen),
        grid=(t // tt,),
        in_specs=[
            pl.BlockSpec((tt, d), lambda i: (i, 0)),
            pl.BlockSpec((None, None, 6, d), _mod_row_map(layer, n_ctx_tiles, tiles_per_lat_seq)),
            pl.BlockSpec((1, d), const),
            pl.BlockSpec((d, dz), const),
            pl.BlockSpec((d, dx), const),
            pl.BlockSpec((d, 2 * LANES), const),
            pl.BlockSpec((CONV_W, dx), const),
            pl.BlockSpec((1, dx), const),
            pl.BlockSpec((2, LANES), const),
        ],
        out_specs=[
            pl.BlockSpec((tt, dz), lambda i: (i, 0)),
            pl.BlockSpec((tt, dx), lambda i: (i, 0)),
            pl.BlockSpec((2, tt, LANES), lambda i: (0, i, 0)),
        ],
        out_shape=[
            jax.ShapeDtypeStruct((t, dz), F32),
            jax.ShapeDtypeStruct((t, dx), F32),
            jax.ShapeDtypeStruct((2, t, LANES), F32),
        ],
        compiler_params=_cparams(("arbitrary",)),
        name="ssd_in",
    )(x, mod4, g, w_z, w_x, w_dt, conv_w, conv_b, dt_bias)


def _ssd_scan_kernel(n_heads, ctx_chunks, ctx_nc, lat_nc, xbc_f, xbc_b, dt_f, dt_b, a_ref, e_ref,
                     h0_f, h0_b, y_f, y_b, st_f, st_b, state_f, state_b):
    step = pl.program_id(0)
    geometry = (n_heads, ctx_chunks, ctx_nc, lat_nc)
    dirs = ((True, step, xbc_f, dt_f, a_ref[0], e_ref, h0_f, y_f, st_f, state_f),
            (False, pl.num_programs(0) - 1 - step, xbc_b, dt_b, a_ref[1], e_ref, h0_b, y_b, st_b,
             state_b))
    for phase in ("init", "main", "emit"):
        for args in dirs:
            _scan_direction(phase, geometry, *args)


def _scan_direction(phase, geometry, fwd, chunk, xbc_ref, dt_ref, a_row, e_ref, h0_ref, y_ref,
                    st_ref, state_s):
    n_heads, ctx_chunks, ctx_nc, lat_nc = geometry
    q = SSD_CHUNK
    p = SSD_HEAD_DIM
    d_inner = n_heads * p
    gn = SSD_GROUPS * SSD_STATE
    heads_per_group = n_heads // SSD_GROUPS

    is_ctx = chunk < ctx_chunks
    seq_nc = jnp.where(is_ctx, ctx_nc, lat_nc)
    local = jnp.where(is_ctx, lax.rem(chunk, ctx_nc), lax.rem(chunk - ctx_chunks, lat_nc))
    seq_first = local == (0 if fwd else seq_nc - 1)
    seq_last = local == (seq_nc - 1 if fwd else 0)

    if phase == "init":
        @pl.when(jnp.logical_and(seq_first, is_ctx))
        def _():
            state_s[...] = jnp.zeros_like(state_s)

        @pl.when(jnp.logical_and(seq_first, jnp.logical_not(is_ctx)))
        def _():
            for pair in range(n_heads // 2):
                state_s[pair] = h0_ref[pair].T
        return

    if phase == "emit":
        @pl.when(jnp.logical_and(seq_last, is_ctx))
        def _():
            for pair in range(n_heads // 2):
                st_ref[pair] = state_s[pair].T
        return

    row = lax.broadcasted_iota(jnp.int32, (q, q), 0)
    col = lax.broadcasted_iota(jnp.int32, (q, q), 1)
    mask = (col <= row) if fwd else (col >= row)
    tri = jnp.where(mask, 1.0, 0.0)

    dt = dt_ref[...]
    dta = dt * (-jnp.exp(a_row))
    cum = jnp.dot(tri, dta, preferred_element_type=F32, precision=lax.Precision.HIGHEST)
    cum_t = cum.T
    tot = cum[q - 1:q, :] if fwd else cum[0:1, :]

    def per_lane(m):
        e = e_ref[...]
        hi = m.astype(BF16)
        lo = (m - hi.astype(F32)).astype(BF16)
        return jnp.dot(hi, e, preferred_element_type=F32) + jnp.dot(lo, e, preferred_element_type=F32)

    exp_cum = jnp.exp(cum)
    dt_l = per_lane(dt)
    dt_end_l = per_lane(dt * jnp.exp(tot - cum))
    exp_cum_l = per_lane(exp_cum)
    exp_tot_l = exp_cum_l[q - 1:q, :] if fwd else exp_cum_l[0:1, :]

    first = lax.broadcasted_iota(jnp.int32, (q, 2 * p), 1) < p

    for g in range(SSD_GROUPS):
        bm_f = xbc_ref[:, d_inner + g * SSD_STATE:d_inner + (g + 1) * SSD_STATE]
        bm = bm_f.astype(BF16)
        bm_t = bm_f.T.astype(BF16)
        cm = xbc_ref[:, d_inner + gn + g * SSD_STATE:d_inner + gn + (g + 1) * SSD_STATE].astype(BF16)
        cb = lax.dot_general(cm, bm, (((1,), (1,)), ((), ())), preferred_element_type=F32)
        for pr in range(heads_per_group // 2):
            hd = g * heads_per_group + 2 * pr
            pair = hd // 2
            lanes = slice(hd * p, (hd + 2) * p)
            xs = xbc_ref[:, lanes]
            xdt = xs * dt_l[:, lanes]
            ms = []
            for k in range(2):
                seg = cum[:, hd + k:hd + k + 1] - cum_t[hd + k:hd + k + 1, :]
                ms.append((jnp.where(mask, jnp.exp(seg), 0.0) * cb).astype(BF16))
            rhs = jnp.concatenate([jnp.where(first, xdt, 0.0), jnp.where(first, 0.0, xdt)],
                                  axis=0).astype(BF16)
            y = jnp.dot(jnp.concatenate(ms, axis=1), rhs, preferred_element_type=F32)
            h_prev = state_s[pair]
            y_off = jnp.dot(cm, h_prev.astype(BF16), preferred_element_type=F32)
            y_ref[:, lanes] = y + y_off * exp_cum_l[:, lanes]
            xdtw = (xs * dt_end_l[:, lanes]).astype(BF16)
            upd = jnp.dot(bm_t, xdtw, preferred_element_type=F32)
            state_s[pair] = exp_tot_l[:, lanes] * h_prev + upd


def _ssd_scan(xbc, dt, a_log, h0, *, n_heads, n_ctx_seq, ctx_nc, lat_nc):
    t, dx = xbc.shape
    d_inner = n_heads * SSD_HEAD_DIM
    n_pairs = n_heads // 2
    q = SSD_CHUNK
    n_chunks = t // q
    ctx_chunks = n_ctx_seq * ctx_nc
    state_tile = (n_pairs, 2 * SSD_HEAD_DIM, SSD_STATE)
    assert 2 * SSD_HEAD_DIM == SSD_STATE
    head_lanes = (jnp.arange(d_inner)[None, :] // SSD_HEAD_DIM == jnp.arange(LANES)[:, None]).astype(BF16)

    fwd_chunk = lambda s: s
    bwd_chunk = lambda s: n_chunks - 1 - s

    def h0_seq(chunk):
        return jnp.maximum(chunk - ctx_chunks, 0) // lat_nc

    def st_seq(chunk):
        return jnp.minimum(chunk, ctx_chunks - 1) // ctx_nc

    y_shape = jax.ShapeDtypeStruct((t, d_inner), F32)
    st_shape = jax.ShapeDtypeStruct((n_ctx_seq,) + state_tile, F32)
    y_f, y_b, st_f, st_b = pl.pallas_call(
        functools.partial(_ssd_scan_kernel, n_heads, ctx_chunks, ctx_nc, lat_nc),
        grid=(n_chunks,),
        in_specs=[
            pl.BlockSpec((q, dx), lambda s: (fwd_chunk(s), 0)),
            pl.BlockSpec((q, dx), lambda s: (bwd_chunk(s), 0)),
            pl.BlockSpec((None, q, LANES), lambda s: (0, fwd_chunk(s), 0)),
            pl.BlockSpec((None, q, LANES), lambda s: (1, bwd_chunk(s), 0)),
            pl.BlockSpec((2, 1, LANES), lambda s: (0, 0, 0)),
            pl.BlockSpec((LANES, d_inner), lambda s: (0, 0)),
            pl.BlockSpec((None, None) + state_tile, lambda s: (h0_seq(fwd_chunk(s)), 0, 0, 0, 0)),
            pl.BlockSpec((None, None) + state_tile, lambda s: (h0_seq(bwd_chunk(s)), 1, 0, 0, 0)),
        ],
        out_specs=[
            pl.BlockSpec((q, d_inner), lambda s: (fwd_chunk(s), 0)),
            pl.BlockSpec((q, d_inner), lambda s: (bwd_chunk(s), 0)),
            pl.BlockSpec((None,) + state_tile, lambda s: (st_seq(fwd_chunk(s)), 0, 0, 0)),
            pl.BlockSpec((None,) + state_tile, lambda s: (st_seq(bwd_chunk(s)), 0, 0, 0)),
        ],
        out_shape=[y_shape, y_shape, st_shape, st_shape],
        scratch_shapes=[pltpu.VMEM(state_tile, F32), pltpu.VMEM(state_tile, F32)],
        compiler_params=_cparams(("arbitrary",)),
        name="ssd_scan",
    )(xbc, xbc, dt, dt, a_log, head_lanes, h0, h0)
    return y_f, y_b, jnp.stack([st_f, st_b], axis=1)


def _ssd_out_kernel(x_ref, mod_ref, yf_ref, yb_ref, xs_ref, z_ref, dskip_ref, ng_ref, wout_ref,
                    o_ref):
    y = yf_ref[...] + yb_ref[...] + dskip_ref[...] * xs_ref[...]
    yz = y * _silu(z_ref[...])
    ms = jnp.mean(yz * yz, axis=-1, keepdims=True)
    yn = (yz * lax.rsqrt(ms + EPS) * ng_ref[...]).astype(BF16)
    mix = jnp.dot(yn, wout_ref[...], preferred_element_type=F32)
    o_ref[...] = x_ref[...] + mod_ref[...][2:3] * mix


def _ssd_out(x, mod4, layer, y_f, y_b, xbc, z, d_skip, norm_g, w_out, *, tt, n_ctx_tiles,
             tiles_per_lat_seq):
    t, d = x.shape
    di = z.shape[1]
    const = lambda i: (0, 0)
    return pl.pallas_call(
        _ssd_out_kernel,
        grid=(t // tt,),
        in_specs=[
            pl.BlockSpec((tt, d), lambda i: (i, 0)),
            pl.BlockSpec((None, None, 6, d), _mod_row_map(layer, n_ctx_tiles, tiles_per_lat_seq)),
            pl.BlockSpec((tt, di), lambda i: (i, 0)),
            pl.BlockSpec((tt, di), lambda i: (i, 0)),
            pl.BlockSpec((tt, di), lambda i: (i, 0)),
            pl.BlockSpec((tt, di), lambda i: (i, 0)),
            pl.BlockSpec((1, di), const),
            pl.BlockSpec((1, di), const),
            pl.BlockSpec((di, d), const),
        ],
        out_specs=pl.BlockSpec((tt, d), lambda i: (i, 0)),
        out_shape=jax.ShapeDtypeStruct((t, d), F32),
        compiler_params=_cparams(("arbitrary",)),
        name="ssd_out",
    )(x, mod4, y_f, y_b, xbc, z, d_skip, norm_g, w_out)


def _pick_tile(candidates, *lengths):
    for tt in candidates:
        if all(n % tt == 0 for n in lengths):
            return tt
    raise ValueError(f"no tile in {candidates} divides {lengths}")


def kernel(x_prompt, x_sample, state_ssm, c, c_ctx, norm_mix_g, norm_ffn_g, norm_f_g, ada_w, ada_b, sc_w_in, sc_conv_w, sc_w_out, ssd_w_in, ssd_conv_w, ssd_conv_b, ssd_dt_bias, ssd_a_log, ssd_d, ssd_norm_g, ssd_w_out, peer_wq, peer_keys, peer_u, peer_v):
    n_ctx_seq, ctx_len, d = x_prompt.shape
    n_lat_seq, lat_len, _ = x_sample.shape
    depth = ada_w.shape[0]
    t_ctx = n_ctx_seq * ctx_len
    d_inner = ssd_norm_g.shape[1]
    n_heads = d_inner // SSD_HEAD_DIM
    conv_dim = ssd_conv_w.shape[2]
    assert ctx_len & (ctx_len - 1) == 0 and ctx_len % SSD_CHUNK == 0 and lat_len % SSD_CHUNK == 0
    assert n_heads % (2 * SSD_GROUPS) == 0 and n_heads <= LANES

    x = None
    n_cond = 1 + n_lat_seq
    cond_rows = -(-n_cond // SUBLANES) * SUBLANES
    cond = jnp.concatenate([c_ctx[None], c, jnp.zeros((cond_rows - n_cond, d), F32)], axis=0)
    mod4 = _adaln(cond, ada_w, ada_b).reshape(depth, cond_rows, 6, d)

    def tiling(tt):
        return dict(tt=tt, n_ctx_tiles=t_ctx // tt, tiles_per_lat_seq=lat_len // tt)

    tt_mix = _pick_tile((512, 256), t_ctx, lat_len)
    tt_route = _pick_tile((ROUTE_TILE,), t_ctx, lat_len)
    tt_dense = _pick_tile((512, 256), t_ctx, lat_len)
    assert tt_mix % ctx_len == 0 and tt_mix % GRID_W == 0

    u_all, v_all = peer_u.astype(BF16).transpose(0, 2, 1), peer_v.astype(BF16)
    states = []
    for i in range(depth):
        j = i // 2
        if i % 2 == 0:
            if x is None:
                x_ctx, x_lat = x_prompt.reshape(t_ctx, d), x_sample.reshape(n_lat_seq * lat_len, d)
            else:
                x_ctx, x_lat = x[:t_ctx], x[t_ctx:]
            x = _conv_mixer(x_ctx, x_lat, mod4, i, norm_mix_g[i][None], sc_w_in[j].astype(BF16), sc_conv_w[j],
                            sc_w_out[j].astype(BF16), ctx_rowlen=ctx_len, **tiling(tt_mix))
        else:
            w_in = ssd_w_in[j]
            w_z = w_in[:, :d_inner].astype(BF16)
            w_x = w_in[:, d_inner:d_inner + conv_dim].astype(BF16)
            w_dt_raw = w_in[:, d_inner + conv_dim:]
            pad = jnp.zeros((d, LANES - n_heads), F32)
            w_dt = jnp.concatenate([w_dt_raw[:, :n_heads], pad, w_dt_raw[:, n_heads:], pad], axis=1).astype(BF16)
            lane_pad = ((0, 0), (0, LANES - n_heads))
            dt_bias = jnp.pad(ssd_dt_bias[j], lane_pad)
            a_log = jnp.pad(ssd_a_log[j], lane_pad)[:, None, :]
            z, xbc, dt = _ssd_in(x, mod4, i, norm_mix_g[i][None], w_z, w_x, w_dt, ssd_conv_w[j],
                                 ssd_conv_b[j][None], dt_bias, ctx_rowlen=ctx_len, **tiling(tt_mix))
            n_pairs = n_heads // 2
            h0 = state_ssm[:, j].reshape(n_lat_seq, 2, n_pairs, 2 * SSD_HEAD_DIM, SSD_STATE)
            y_f, y_b, st = _ssd_scan(xbc, dt, a_log, h0, n_heads=n_heads, n_ctx_seq=n_ctx_seq,
                                     ctx_nc=ctx_len // SSD_CHUNK, lat_nc=lat_len // SSD_CHUNK)
            states.append(st.reshape(n_ctx_seq, 2, n_heads, SSD_HEAD_DIM, SSD_STATE))
            d_skip = jnp.repeat(ssd_d[j][0] + ssd_d[j][1], SSD_HEAD_DIM)[None]
            x = _ssd_out(x, mod4, i, y_f, y_b, xbc, z, d_skip, ssd_norm_g[j][None],
                         ssd_w_out[j].astype(BF16), **tiling(tt_mix))
        keys = peer_keys[i].reshape(2 * PEER_HEADS, N_KEYS, PEER_HALF).astype(BF16)
        i1, i2, gate = _peer_route(x, mod4, i, norm_ffn_g[i][None], peer_wq[i].astype(BF16), keys,
                                   **tiling(tt_route))
        x = _peer_dense(x, mod4, i, norm_ffn_g[i][None], i1, i2, gate, u_all, v_all,
                        norm_f_g[None], keys_per_blk=16,
                        final_norm=(i == depth - 1), **tiling(tt_dense))

    y_ctx, y_lat = x
    return (y_ctx.reshape(n_ctx_seq, ctx_len, d), y_lat.reshape(n_lat_seq, lat_len, d),
            jnp.stack(states, axis=1))
```

```python
import functools
import math

import jax
import jax.numpy as jnp
from jax import lax
from jax.experimental import pallas as pl
from jax.experimental.pallas import tpu as pltpu

F32 = jnp.float32
BF16 = jnp.bfloat16
EPS = 1e-6

LANES = 128
SUBLANES = 8
VMEM_LIMIT_BYTES = 60 * 1024 * 1024

GRID_W = 64
CONV_W = 3
SSD_HEAD_DIM = 64
SSD_STATE = 128
SSD_GROUPS = 8
SSD_CHUNK = 128
N_KEYS = 128
PEER_HEADS = 8
PEER_TOPK = 16
PEER_HALF = 128

NEG_INF = float("-inf")


def _cparams(sem):
    return pltpu.CompilerParams(dimension_semantics=sem, vmem_limit_bytes=VMEM_LIMIT_BYTES)


def _norm_mod(x, g, scl, sh):
    ms = jnp.mean(x * x, axis=-1, keepdims=True)
    return x * lax.rsqrt(ms + EPS) * g * (1.0 + scl) + sh


def _silu(x):
    return x * (1.0 / (1.0 + jnp.exp(-x)))


def _mod_row_map(layer, n_ctx_tiles, tiles_per_lat_seq):
    def index_map(i, *_):
        row = jnp.where(i < n_ctx_tiles, 0, 1 + (i - n_ctx_tiles) // tiles_per_lat_seq)
        return (layer, row, 0, 0)
    return index_map


def _shift_rows(u, pos, rowlen):
    n = u.shape[0]
    up = pltpu.roll(u, 1, axis=0)
    dn = pltpu.roll(u, n - 1, axis=0)
    up = jnp.where(pos == 0, 0.0, up)
    dn = jnp.where(pos == rowlen - 1, 0.0, dn)
    return up, dn


def _adaln_kernel(c_ref, w_ref, b_ref, o_ref):
    s = _silu(c_ref[...])
    o_ref[...] = jnp.dot(s, w_ref[...], preferred_element_type=F32,
                         precision=lax.Precision.HIGHEST) + b_ref[...]


def _adaln(cond, ada_w, ada_b):
    depth, d, n = ada_w.shape
    rows = cond.shape[0]
    tn = 1536
    return pl.pallas_call(
        _adaln_kernel,
        grid=(depth, n // tn),
        in_specs=[
            pl.BlockSpec((rows, d), lambda l, j: (0, 0)),
            pl.BlockSpec((None, d, tn), lambda l, j: (l, 0, j)),
            pl.BlockSpec((None, 1, tn), lambda l, j: (l, 0, j)),
        ],
        out_specs=pl.BlockSpec((None, rows, tn), lambda l, j: (l, 0, j)),
        out_shape=jax.ShapeDtypeStruct((depth, rows, n), F32),
        compiler_params=_cparams(("arbitrary", "arbitrary")),
        name="adaln",
    )(cond, ada_w, ada_b.reshape(depth, 1, n))


def _conv_mixer_kernel(n_ctx_tiles, ctx_rowlen, xc_ref, xl_ref, mod_ref, g_ref, win_ref, cw_ref,
                       wout_ref, o_ref):
    i = pl.program_id(0)
    x = jnp.where(i < n_ctx_tiles, xc_ref[...], xl_ref[...])
    d = x.shape[1]
    mod = mod_ref[...]
    hn = _norm_mod(x, g_ref[...], mod[1:2], mod[0:1]).astype(BF16)
    p = jnp.dot(hn, win_ref[...], preferred_element_type=F32)
    bg, cg, xv = p[:, :d], p[:, d:2 * d], p[:, 2 * d:]
    u = cg * xv
    rowlen = jnp.where(i < n_ctx_tiles, ctx_rowlen, GRID_W)
    pos = lax.broadcasted_iota(jnp.int32, (x.shape[0], 1), 0) & (rowlen - 1)
    up, dn = _shift_rows(u, pos, rowlen)
    cw = cw_ref[...]
    y = up * cw[0:1] + u * cw[1:2] + dn * cw[2:3]
    mix = jnp.dot((bg * y).astype(BF16), wout_ref[...], preferred_element_type=F32)
    o_ref[...] = x + mod[2:3] * mix


def _conv_mixer(x_ctx, x_lat, mod4, layer, g, w_in, conv_w, w_out, *, tt, n_ctx_tiles, ctx_rowlen,
                tiles_per_lat_seq):
    d = x_ctx.shape[1]
    t = x_ctx.shape[0] + x_lat.shape[0]
    return pl.pallas_call(
        functools.partial(_conv_mixer_kernel, n_ctx_tiles, ctx_rowlen),
        grid=(t // tt,),
        in_specs=[
            pl.BlockSpec((tt, d), lambda i: (jnp.minimum(i, n_ctx_tiles - 1), 0)),
            pl.BlockSpec((tt, d), lambda i: (jnp.maximum(i - n_ctx_tiles, 0), 0)),
            pl.BlockSpec((None, None, 6, d), _mod_row_map(layer, n_ctx_tiles, tiles_per_lat_seq)),
            pl.BlockSpec((1, d), lambda i: (0, 0)),
            pl.BlockSpec((d, 3 * d), lambda i: (0, 0)),
            pl.BlockSpec((CONV_W, d), lambda i: (0, 0)),
            pl.BlockSpec((d, d), lambda i: (0, 0)),
        ],
        out_specs=pl.BlockSpec((tt, d), lambda i: (i, 0)),
        out_shape=jax.ShapeDtypeStruct((t, d), F32),
        compiler_params=_cparams(("arbitrary",)),
        name="conv_mixer",
    )(x_ctx, x_lat, mod4, g, w_in, conv_w, w_out)


ROUTE_TILE = SUBLANES * LANES


def _oddeven_merge_sort_pairs(n):
    pairs = []
    p = 1
    while p < n:
        k = p
        while k >= 1:
            for j in range(k % p, n - k, 2 * k):
                for i in range(min(k, n - j - k)):
                    if (i + j) // (2 * p) == (i + j + k) // (2 * p):
                        pairs.append((i + j, i + j + k))
            k //= 2
        p *= 2
    return pairs


_SORT16 = _oddeven_merge_sort_pairs(PEER_TOPK)


def _goes_first(a, ia, b, ib):
    return (a > b) | ((a == b) & (ia < ib))


def _compare_exchange(v, ix, i, j, ids_ordered=False):
    a, b, ia, ib = v[i], v[j], ix[i], ix[j]
    f = (a >= b) if ids_ordered else _goes_first(a, ia, b, ib)
    v[i], v[j] = jnp.where(f, a, b), jnp.where(f, b, a)
    ix[i], ix[j] = jnp.where(f, ia, ib), jnp.where(f, ib, ia)


def _sort16(v, ix):
    for n, (i, j) in enumerate(_SORT16):
        _compare_exchange(v, ix, i, j, ids_ordered=n < PEER_TOPK // 2)


def _bitonic_sort16(v, ix):
    d = PEER_TOPK // 2
    while d >= 1:
        for i in range(PEER_TOPK):
            if i & d == 0:
                _compare_exchange(v, ix, i, i + d)
        d //= 2


def _merge_top16(rv, ri, xv, xi, sort=True):
    for k in range(len(xv)):
        i = PEER_TOPK - 1 - k
        f = _goes_first(rv[i], ri[i], xv[k], xi[k])
        rv[i] = jnp.where(f, rv[i], xv[k])
        ri[i] = jnp.where(f, ri[i], xi[k])
    if sort:
        _bitonic_sort16(rv, ri)


def _vreg_rows(k):
    return pl.ds(k * SUBLANES, SUBLANES)


def _peer_route_kernel(x_ref, mod_ref, g_ref, wq_ref, keys_ref, i1_ref, i2_ref, gate_ref,
                       hn_s, q_s, sc_s, topv_s, topi_s, e1_s, e2_s, gt_s):
    mod = mod_ref[...]
    hn_s[...] = _norm_mod(x_ref[...], g_ref[...], mod[4:5], mod[3:4]).astype(BF16)
    for h in range(PEER_HEADS):
        qh = jnp.dot(hn_s[...], wq_ref[:, 2 * h * PEER_HALF:2 * (h + 1) * PEER_HALF],
                     preferred_element_type=F32).astype(BF16)
        q_s[2 * h] = qh[:, :PEER_HALF]
        q_s[2 * h + 1] = qh[:, PEER_HALF:]

    def stage1(hp, carry):
        st = lax.dot_general(keys_ref[hp], q_s[hp], (((1,), (1,)), ((), ())),
                             preferred_element_type=F32)
        for c in range(SUBLANES):
            sc_s[pl.ds(c, N_KEYS, stride=SUBLANES), :] = st[:, c * LANES:(c + 1) * LANES]
        rv = ri = None
        for grp in range(N_KEYS // PEER_TOPK):
            v = [sc_s[_vreg_rows(grp * PEER_TOPK + i), :] for i in range(PEER_TOPK)]
            ix = [float(grp * PEER_TOPK + i) for i in range(PEER_TOPK)]
            _sort16(v, ix)
            if rv is None:
                rv, ri = v, ix
            else:
                _merge_top16(rv, ri, v, ix)
        for k in range(PEER_TOPK):
            topv_s[hp, _vreg_rows(k), :] = rv[k]
            topi_s[hp, _vreg_rows(k), :] = ri[k]
        return carry

    lax.fori_loop(0, 2 * PEER_HEADS, stage1, 0)

    n_wide = PEER_TOPK // 2
    cand_lists = [[(a, b) for b in range(PEER_TOPK // (a + 1))] for a in range(n_wide)]
    cand_lists.append([(a, 0) for a in range(n_wide, PEER_TOPK)])

    def stage2(h, carry):
        s1 = [topv_s[2 * h, _vreg_rows(k), :] for k in range(PEER_TOPK)]
        s2 = [topv_s[2 * h + 1, _vreg_rows(k), :] for k in range(PEER_TOPK)]
        rv = rf = None
        for n, pairs in enumerate(cand_lists):
            v = [s1[a] + s2[b] for a, b in pairs]
            flat = [float(a * PEER_TOPK + b) for a, b in pairs]
            if rv is None:
                rv, rf = v, flat
            else:
                _merge_top16(rv, rf, v, flat, sort=n < len(cand_lists) - 1)
        top = s1[0] + s2[0]
        ex = [jnp.exp(v - top) for v in rv]
        denom = ex[0]
        for e in ex[1:]:
            denom = denom + e
        inv = 1.0 / denom
        k1 = [topi_s[2 * h, _vreg_rows(k), :] for k in range(PEER_TOPK)]
        k2 = [topi_s[2 * h + 1, _vreg_rows(k), :] for k in range(PEER_TOPK)]
        for k in range(PEER_TOPK):
            pos1 = jnp.floor(rf[k] * (1.0 / PEER_TOPK))
            pos2 = rf[k] - pos1 * PEER_TOPK
            e1 = jnp.zeros_like(top)
            e2 = jnp.zeros_like(top)
            for a in range(PEER_TOPK):
                e1 = jnp.where(pos1 == a, k1[a], e1)
                e2 = jnp.where(pos2 == a, k2[a], e2)
            rows = pl.ds(pl.multiple_of((h * PEER_TOPK + k) * SUBLANES, SUBLANES), SUBLANES)
            e1_s[rows, :] = e1
            e2_s[rows, :] = e2
            gt_s[rows, :] = ex[k] * inv
        return carry

    lax.fori_loop(0, PEER_HEADS, stage2, 0)

    for src, dst in ((e1_s, i1_ref), (e2_s, i2_ref), (gt_s, gate_ref)):
        for c in range(SUBLANES):
            dst[c * LANES:(c + 1) * LANES, :] = src[pl.ds(c, LANES, stride=SUBLANES), :].T


def _peer_route(x, mod4, layer, g, wq, keys, *, tt, n_ctx_tiles, tiles_per_lat_seq):
    t, d = x.shape
    nsel = PEER_HEADS * PEER_TOPK
    assert tt == ROUTE_TILE and nsel == LANES
    out = jax.ShapeDtypeStruct((t, nsel), F32)
    vreg_table = pltpu.VMEM((nsel * SUBLANES, LANES), F32)
    return pl.pallas_call(
        _peer_route_kernel,
        grid=(t // tt,),
        scratch_shapes=[
            pltpu.VMEM((tt, d), BF16),
            pltpu.VMEM((2 * PEER_HEADS, tt, PEER_HALF), BF16),
            pltpu.VMEM((N_KEYS * SUBLANES, LANES), F32),
            pltpu.VMEM((2 * PEER_HEADS, PEER_TOPK * SUBLANES, LANES), F32),
            pltpu.VMEM((2 * PEER_HEADS, PEER_TOPK * SUBLANES, LANES), F32),
            vreg_table, vreg_table, vreg_table,
        ],
        in_specs=[
            pl.BlockSpec((tt, d), lambda i: (i, 0)),
            pl.BlockSpec((None, None, 6, d), _mod_row_map(layer, n_ctx_tiles, tiles_per_lat_seq)),
            pl.BlockSpec((1, d), lambda i: (0, 0)),
            pl.BlockSpec(wq.shape, lambda i: (0, 0)),
            pl.BlockSpec(keys.shape, lambda i: (0, 0, 0)),
        ],
        out_specs=[pl.BlockSpec((tt, nsel), lambda i: (i, 0))] * 3,
        out_shape=[out, out, out],
        compiler_params=_cparams(("arbitrary",)),
        name="peer_route",
    )(x, mod4, g, wq, keys)


_HIGH_HALF = 0xFFFF0000


def _pack_bf16_pair(lo, hi):
    lo_bits = lax.bitcast_convert_type(lo.astype(BF16).astype(F32), jnp.uint32)
    hi_bits = lax.bitcast_convert_type(hi.astype(BF16).astype(F32), jnp.uint32)
    return (lo_bits >> 16) | (hi_bits & jnp.uint32(_HIGH_HALF))


def _unpack_bf16_pair(words, high):
    bits = (words & jnp.uint32(_HIGH_HALF)) if high else (words << 16)
    return lax.bitcast_convert_type(bits, F32)


def _gate_pitch(tt):
    return tt + SUBLANES


def _peer_dense_kernel(final, tt, keys_per_blk, n_ctx_tiles, x_ref, mod_ref, g_ref, i1_ref, i2_ref,
                       gate_ref, u_ref, v_ref, gf_ref, *refs):
    *out_refs, hn_s, w_s, acc_s = refs
    eb = pl.program_id(1)
    pitch = _gate_pitch(tt)
    half = N_KEYS // 2

    @pl.when(eb == 0)
    def _():
        mod = mod_ref[...]
        hn_s[...] = _norm_mod(x_ref[...], g_ref[...], mod[4:5], mod[3:4]).astype(BF16)
        acc_s[...] = jnp.zeros_like(acc_s)
        key2_ids = lax.broadcasted_iota(jnp.int32, (N_KEYS, N_KEYS), 0)
        r = key2_ids & (half - 1)
        key1_ids = (((r >> 2) << 3) + (r & 3) + jnp.where(key2_ids >= half, 4, 0)).astype(F32)
        key2_ids = key2_ids.astype(F32)

        def build(t, carry):
            i1 = i1_ref[pl.ds(t, 1), :]
            i2 = i2_ref[pl.ds(t, 1), :]
            gt = gate_ref[pl.ds(t, 1), :]
            p = jnp.where(i1 == key1_ids, gt, 0.0).astype(BF16)
            q = jnp.where(i2 == key2_ids, 1.0, 0.0).astype(BF16)
            w = lax.dot_general(p, q, (((1,), (1,)), ((), ())), preferred_element_type=F32)
            w_s[pl.ds(t, half, stride=pitch), :] = _pack_bf16_pair(w[:half], w[half:])
            return carry

        lax.fori_loop(0, tt, build, 0, unroll=64)

    act = jnp.dot(hn_s[...], u_ref[...], preferred_element_type=F32)
    hs = [None] * keys_per_blk
    for oct_ in range(keys_per_blk // SUBLANES):
        for la in range(SUBLANES // 2):
            slab = (eb * (keys_per_blk // SUBLANES) + oct_) * (SUBLANES // 2) + la
            packed = w_s[pl.ds(pl.multiple_of(slab * pitch, SUBLANES), tt), :]
            for hi in range(2):
                a = oct_ * SUBLANES + hi * (SUBLANES // 2) + la
                wa = _unpack_bf16_pair(packed, hi)
                xa = act[:, a * N_KEYS:(a + 1) * N_KEYS]
                gelu = 0.5 * xa * (1.0 + lax.erf(xa * math.sqrt(0.5)))
                hs[a] = (gelu * wa).astype(BF16)
    h = jnp.concatenate(hs, axis=1)
    acc_s[...] += jnp.dot(h, v_ref[...], preferred_element_type=F32)

    @pl.when(eb == pl.num_programs(1) - 1)
    def _():
        y = x_ref[...] + mod_ref[...][5:6] * acc_s[...]
        if not final:
            out_refs[0][...] = y
        else:
            ms = jnp.mean(y * y, axis=-1, keepdims=True)
            y = y * lax.rsqrt(ms + EPS) * gf_ref[...]
            is_ctx = pl.program_id(0) < n_ctx_tiles

            @pl.when(is_ctx)
            def _():
                out_refs[0][...] = y

            @pl.when(jnp.logical_not(is_ctx))
            def _():
                out_refs[1][...] = y


def _peer_dense(x, mod4, layer, g, i1, i2, gate, u, v, g_final, *, tt, keys_per_blk,
                n_ctx_tiles, tiles_per_lat_seq, final_norm):
    t, d = x.shape
    n_exp = v.shape[1]
    eblk = keys_per_blk * N_KEYS
    nsel = i1.shape[1]
    pitch = _gate_pitch(tt)
    mod_map = _mod_row_map(layer, n_ctx_tiles, tiles_per_lat_seq)
    if final_norm:
        t_ctx = n_ctx_tiles * tt
        out_specs = [pl.BlockSpec((tt, d), lambda i, e: (jnp.minimum(i, n_ctx_tiles - 1), 0)),
                     pl.BlockSpec((tt, d), lambda i, e: (jnp.maximum(i - n_ctx_tiles, 0), 0))]
        out_shape = [jax.ShapeDtypeStruct((t_ctx, d), F32), jax.ShapeDtypeStruct((t - t_ctx, d), F32)]
    else:
        out_specs = pl.BlockSpec((tt, d), lambda i, e: (i, 0))
        out_shape = jax.ShapeDtypeStruct((t, d), F32)
    return pl.pallas_call(
        functools.partial(_peer_dense_kernel, final_norm, tt, keys_per_blk, n_ctx_tiles),
        grid=(t // tt, n_exp // eblk),
        in_specs=[
            pl.BlockSpec((tt, d), lambda i, e: (i, 0), pipeline_mode=pl.Buffered(1)),
            pl.BlockSpec((None, None, 6, d), mod_map),
            pl.BlockSpec((1, d), lambda i, e: (0, 0)),
            pl.BlockSpec((tt, nsel), lambda i, e: (i, 0), pipeline_mode=pl.Buffered(1)),
            pl.BlockSpec((tt, nsel), lambda i, e: (i, 0), pipeline_mode=pl.Buffered(1)),
            pl.BlockSpec((tt, nsel), lambda i, e: (i, 0), pipeline_mode=pl.Buffered(1)),
            pl.BlockSpec((None, d, eblk), lambda i, e: (layer, 0, e)),
            pl.BlockSpec((None, eblk, d), lambda i, e: (layer, e, 0)),
            pl.BlockSpec((1, d), lambda i, e: (0, 0)),
        ],
        out_specs=out_specs,
        out_shape=out_shape,
        scratch_shapes=[
            pltpu.VMEM((tt, d), BF16),
            pltpu.VMEM((N_KEYS // 2 * pitch, N_KEYS), jnp.uint32),
            pltpu.VMEM((tt, d), F32),
        ],
        compiler_params=_cparams(("arbitrary", "arbitrary")),
        name="peer_dense",
    )(x, mod4, g, i1, i2, gate, u, v, g_final)


def _softplus(x):
    return jnp.maximum(x, 0.0) + jnp.log(1.0 + jnp.exp(-jnp.abs(x)))


def _ssd_in_kernel(n_ctx_tiles, ctx_rowlen, x_ref, mod_ref, g_ref, wz_ref, wx_ref, wdt_ref,
                   cw_ref, cb_ref, dtb_ref, z_ref, xbc_ref, dt_ref):
    i = pl.program_id(0)
    x = x_ref[...]
    mod = mod_ref[...]
    hn = _norm_mod(x, g_ref[...], mod[1:2], mod[0:1]).astype(BF16)
    z_ref[...] = jnp.dot(hn, wz_ref[...], preferred_element_type=F32)
    rowlen = jnp.where(i < n_ctx_tiles, ctx_rowlen, GRID_W)
    pos = lax.broadcasted_iota(jnp.int32, (x.shape[0], 1), 0) & (rowlen - 1)
    chunk = 4 * LANES
    for c0 in range(0, xbc_ref.shape[1], chunk):
        cols = slice(c0, c0 + chunk)
        xbc = jnp.dot(hn, wx_ref[:, cols], preferred_element_type=F32)
        up, dn = _shift_rows(xbc, pos, rowlen)
        cw = cw_ref[:, cols]
        xbc_ref[:, cols] = _silu(up * cw[0:1] + xbc * cw[1:2] + dn * cw[2:3] + cb_ref[:, cols])
    dt = jnp.dot(hn, wdt_ref[...], preferred_element_type=F32)
    dtb = dtb_ref[...]
    dt_ref[0] = _softplus(dt[:, :LANES] + dtb[0:1])
    dt_ref[1] = _softplus(dt[:, LANES:] + dtb[1:2])


def _ssd_in(x, mod4, layer, g, w_z, w_x, w_dt, conv_w, conv_b, dt_bias, *, tt, n_ctx_tiles,
            ctx_rowlen, tiles_per_lat_seq):
    t, d = x.shape
    dz, dx = w_z.shape[1], w_x.shape[1]
    const = lambda i: (0, 0)
    return pl.pallas_call(
        functools.partial(_ssd_in_kernel, n_ctx_tiles, ctx_rowlen),
        grid=(t // tt,),
        in_specs=[
            pl.BlockSpec((tt, d), lambda i: (i, 0)),
            pl.BlockSpec((None, None, 6, d), _mod_row_map(layer, n_ctx_tiles, tiles_per_lat_seq)),
            pl.BlockSpec((1, d), const),
            pl.BlockSpec((d, dz), const),
            pl.BlockSpec((d, dx), const),
            pl.BlockSpec((d, 2 * LANES), const),
            pl.BlockSpec((CONV_W, dx), const),
            pl.BlockSpec((1, dx), const),
            pl.BlockSpec((2, LANES), const),
        ],
        out_specs=[
            pl.BlockSpec((tt, dz), lambda i: (i, 0)),
            pl.BlockSpec((tt, dx), lambda i: (i, 0)),
            pl.BlockSpec((2, tt, LANES), lambda i: (0, i, 0)),
        ],
        out_shape=[
            jax.ShapeDtypeStruct((t, dz), F32),
            jax.ShapeDtypeStruct((t, dx), F32),
            jax.ShapeDtypeStruct((2, t, LANES), F32),
        ],
        compiler_params=_cparams(("arbitrary",)),
        name="ssd_in",
    )(x, mod4, g, w_z, w_x, w_dt, conv_w, conv_b, dt_bias)


def _ssd_scan_kernel(n_heads, ctx_chunks, ctx_nc, lat_nc, xbc_f, xbc_b, dt_f, dt_b, a_ref, e_ref,
                     h0_f, h0_b, y_f, y_b, st_f, st_b, state_f, state_b):
    step = pl.program_id(0)
    geometry = (n_heads, ctx_chunks, ctx_nc, lat_nc)
    dirs = ((True, step, xbc_f, dt_f, a_ref[0], e_ref, h0_f, y_f, st_f, state_f),
            (False, pl.num_programs(0) - 1 - step, xbc_b, dt_b, a_ref[1], e_ref, h0_b, y_b, st_b,
             state_b))
    for phase in ("init", "main", "emit"):
        for args in dirs:
            _scan_direction(phase, geometry, *args)


def _scan_direction(phase, geometry, fwd, chunk, xbc_ref, dt_ref, a_row, e_ref, h0_ref, y_ref,
                    st_ref, state_s):
    n_heads, ctx_chunks, ctx_nc, lat_nc = geometry
    q = SSD_CHUNK
    p = SSD_HEAD_DIM
    d_inner = n_heads * p
    gn = SSD_GROUPS * SSD_STATE
    heads_per_group = n_heads // SSD_GROUPS

    is_ctx = chunk < ctx_chunks
    seq_nc = jnp.where(is_ctx, ctx_nc, lat_nc)
    local = jnp.where(is_ctx, lax.rem(chunk, ctx_nc), lax.rem(chunk - ctx_chunks, lat_nc))
    seq_first = local == (0 if fwd else seq_nc - 1)
    seq_last = local == (seq_nc - 1 if fwd else 0)

    if phase == "init":
        @pl.when(jnp.logical_and(seq_first, is_ctx))
        def _():
            state_s[...] = jnp.zeros_like(state_s)

        @pl.when(jnp.logical_and(seq_first, jnp.logical_not(is_ctx)))
        def _():
            for pair in range(n_heads // 2):
                state_s[pair] = h0_ref[pair].T
        return

    if phase == "emit":
        @pl.when(jnp.logical_and(seq_last, is_ctx))
        def _():
            for pair in range(n_heads // 2):
                st_ref[pair] = state_s[pair].T
        return

    row = lax.broadcasted_iota(jnp.int32, (q, q), 0)
    col = lax.broadcasted_iota(jnp.int32, (q, q), 1)
    mask = (col <= row) if fwd else (col >= row)
    tri = jnp.where(mask, 1.0, 0.0)

    dt = dt_ref[...]
    dta = dt * (-jnp.exp(a_row))
    cum = jnp.dot(tri, dta, preferred_element_type=F32, precision=lax.Precision.HIGHEST)
    cum_t = cum.T
    tot = cum[q - 1:q, :] if fwd else cum[0:1, :]

    def per_lane(m):
        e = e_ref[...]
        hi = m.astype(BF16)
        lo = (m - hi.astype(F32)).astype(BF16)
        return jnp.dot(hi, e, preferred_element_type=F32) + jnp.dot(lo, e, preferred_element_type=F32)

    exp_cum = jnp.exp(cum)
    dt_l = per_lane(dt)
    dt_end_l = per_lane(dt * jnp.exp(tot - cum))
    exp_cum_l = per_lane(exp_cum)
    exp_tot_l = exp_cum_l[q - 1:q, :] if fwd else exp_cum_l[0:1, :]

    first = lax.broadcasted_iota(jnp.int32, (q, 2 * p), 1) < p

    for g in range(SSD_GROUPS):
        bm_f = xbc_ref[:, d_inner + g * SSD_STATE:d_inner + (g + 1) * SSD_STATE]
        bm_t = bm_f.T.astype(BF16)
        cm = xbc_ref[:, d_inner + gn + g * SSD_STATE:d_inner + gn + (g + 1) * SSD_STATE].astype(BF16)
        cb = jnp.dot(cm, bm_t, preferred_element_type=F32)
        for pr in range(heads_per_group // 2):
            hd = g * heads_per_group + 2 * pr
            pair = hd // 2
            lanes = slice(hd * p, (hd + 2) * p)
            xs = xbc_ref[:, lanes]
            xdt = xs * dt_l[:, lanes]
            ms = []
            for k in range(2):
                seg = cum[:, hd + k:hd + k + 1] - cum_t[hd + k:hd + k + 1, :]
                ms.append((jnp.where(mask, jnp.exp(seg), 0.0) * cb).astype(BF16))
            rhs = jnp.concatenate([jnp.where(first, xdt, 0.0), jnp.where(first, 0.0, xdt)],
                                  axis=0).astype(BF16)
            y = jnp.dot(jnp.concatenate(ms, axis=1), rhs, preferred_element_type=F32)
            h_prev = state_s[pair]
            y_off = jnp.dot(cm, h_prev.astype(BF16), preferred_element_type=F32)
            y_ref[:, lanes] = y + y_off * exp_cum_l[:, lanes]
            xdtw = (xs * dt_end_l[:, lanes]).astype(BF16)
            upd = jnp.dot(bm_t, xdtw, preferred_element_type=F32)
            state_s[pair] = exp_tot_l[:, lanes] * h_prev + upd


def _ssd_scan(xbc, dt, a_log, h0, *, n_heads, n_ctx_seq, ctx_nc, lat_nc):
    t, dx = xbc.shape
    d_inner = n_heads * SSD_HEAD_DIM
    n_pairs = n_heads // 2
    q = SSD_CHUNK
    n_chunks = t // q
    ctx_chunks = n_ctx_seq * ctx_nc
    state_tile = (n_pairs, 2 * SSD_HEAD_DIM, SSD_STATE)
    assert 2 * SSD_HEAD_DIM == SSD_STATE
    head_lanes = (jnp.arange(d_inner)[None, :] // SSD_HEAD_DIM == jnp.arange(LANES)[:, None]).astype(BF16)

    fwd_chunk = lambda s: s
    bwd_chunk = lambda s: n_chunks - 1 - s

    def h0_seq(chunk):
        return jnp.maximum(chunk - ctx_chunks, 0) // lat_nc

    def st_seq(chunk):
        return jnp.minimum(chunk, ctx_chunks - 1) // ctx_nc

    y_shape = jax.ShapeDtypeStruct((t, d_inner), F32)
    st_shape = jax.ShapeDtypeStruct((n_ctx_seq,) + state_tile, F32)
    y_f, y_b, st_f, st_b = pl.pallas_call(
        functools.partial(_ssd_scan_kernel, n_heads, ctx_chunks, ctx_nc, lat_nc),
        grid=(n_chunks,),
        in_specs=[
            pl.BlockSpec((q, dx), lambda s: (fwd_chunk(s), 0)),
            pl.BlockSpec((q, dx), lambda s: (bwd_chunk(s), 0)),
            pl.BlockSpec((None, q, LANES), lambda s: (0, fwd_chunk(s), 0)),
            pl.BlockSpec((None, q, LANES), lambda s: (1, bwd_chunk(s), 0)),
            pl.BlockSpec((2, 1, LANES), lambda s: (0, 0, 0)),
            pl.BlockSpec((LANES, d_inner), lambda s: (0, 0)),
            pl.BlockSpec((None, None) + state_tile, lambda s: (h0_seq(fwd_chunk(s)), 0, 0, 0, 0)),
            pl.BlockSpec((None, None) + state_tile, lambda s: (h0_seq(bwd_chunk(s)), 1, 0, 0, 0)),
        ],
        out_specs=[
            pl.BlockSpec((q, d_inner), lambda s: (fwd_chunk(s), 0)),
            pl.BlockSpec((q, d_inner), lambda s: (bwd_chunk(s), 0)),
            pl.BlockSpec((None,) + state_tile, lambda s: (st_seq(fwd_chunk(s)), 0, 0, 0)),
            pl.BlockSpec((None,) + state_tile, lambda s: (st_seq(bwd_chunk(s)), 0, 0, 0)),
        ],
        out_shape=[y_shape, y_shape, st_shape, st_shape],
        scratch_shapes=[pltpu.VMEM(state_tile, F32), pltpu.VMEM(state_tile, F32)],
        compiler_params=_cparams(("arbitrary",)),
        name="ssd_scan",
    )(xbc, xbc, dt, dt, a_log, head_lanes, h0, h0)
    return y_f, y_b, jnp.stack([st_f, st_b], axis=1)


def _ssd_out_kernel(x_ref, mod_ref, yf_ref, yb_ref, xs_ref, z_ref, dskip_ref, ng_ref, wout_ref,
                    o_ref):
    y = yf_ref[...] + yb_ref[...] + dskip_ref[...] * xs_ref[...]
    yz = y * _silu(z_ref[...])
    ms = jnp.mean(yz * yz, axis=-1, keepdims=True)
    yn = (yz * lax.rsqrt(ms + EPS) * ng_ref[...]).astype(BF16)
    mix = jnp.dot(yn, wout_ref[...], preferred_element_type=F32)
    o_ref[...] = x_ref[...] + mod_ref[...][2:3] * mix


def _ssd_out(x, mod4, layer, y_f, y_b, xbc, z, d_skip, norm_g, w_out, *, tt, n_ctx_tiles,
             tiles_per_lat_seq):
    t, d = x.shape
    di = z.shape[1]
    const = lambda i: (0, 0)
    return pl.pallas_call(
        _ssd_out_kernel,
        grid=(t // tt,),
        in_specs=[
            pl.BlockSpec((tt, d), lambda i: (i, 0)),
            pl.BlockSpec((None, None, 6, d), _mod_row_map(layer, n_ctx_tiles, tiles_per_lat_seq)),
            pl.BlockSpec((tt, di), lambda i: (i, 0)),
            pl.BlockSpec((tt, di), lambda i: (i, 0)),
            pl.BlockSpec((tt, di), lambda i: (i, 0)),
            pl.BlockSpec((tt, di), lambda i: (i, 0)),
            pl.BlockSpec((1, di), const),
            pl.BlockSpec((1, di), const),
            pl.BlockSpec((di, d), const),
        ],
        out_specs=pl.BlockSpec((tt, d), lambda i: (i, 0)),
        out_shape=jax.ShapeDtypeStruct((t, d), F32),
        compiler_params=_cparams(("arbitrary",)),
        name="ssd_out",
    )(x, mod4, y_f, y_b, xbc, z, d_skip, norm_g, w_out)


def _pick_tile(candidates, *lengths):
    for tt in candidates:
        if all(n % tt == 0 for n in lengths):
            return tt
    raise ValueError(f"no tile in {candidates} divides {lengths}")


def kernel(x_prompt, x_sample, state_ssm, c, c_ctx, norm_mix_g, norm_ffn_g, norm_f_g, ada_w, ada_b, sc_w_in, sc_conv_w, sc_w_out, ssd_w_in, ssd_conv_w, ssd_conv_b, ssd_dt_bias, ssd_a_log, ssd_d, ssd_norm_g, ssd_w_out, peer_wq, peer_keys, peer_u, peer_v):
    n_ctx_seq, ctx_len, d = x_prompt.shape
    n_lat_seq, lat_len, _ = x_sample.shape
    depth = ada_w.shape[0]
    t_ctx = n_ctx_seq * ctx_len
    d_inner = ssd_norm_g.shape[1]
    n_heads = d_inner // SSD_HEAD_DIM
    conv_dim = ssd_conv_w.shape[2]
    assert ctx_len & (ctx_len - 1) == 0 and ctx_len % SSD_CHUNK == 0 and lat_len % SSD_CHUNK == 0
    assert n_heads % (2 * SSD_GROUPS) == 0 and n_heads <= LANES

    x = None
    n_cond = 1 + n_lat_seq
    cond_rows = -(-n_cond // SUBLANES) * SUBLANES
    cond = jnp.concatenate([c_ctx[None], c, jnp.zeros((cond_rows - n_cond, d), F32)], axis=0)
    mod4 = _adaln(cond, ada_w, ada_b).reshape(depth, cond_rows, 6, d)

    def tiling(tt):
        return dict(tt=tt, n_ctx_tiles=t_ctx // tt, tiles_per_lat_seq=lat_len // tt)

    tt_mix = _pick_tile((512, 256), t_ctx, lat_len)
    tt_route = _pick_tile((ROUTE_TILE,), t_ctx, lat_len)
    tt_dense = _pick_tile((512, 256), t_ctx, lat_len)
    assert tt_mix % ctx_len == 0 and tt_mix % GRID_W == 0

    u_all, v_all = peer_u.astype(BF16).transpose(0, 2, 1), peer_v.astype(BF16)
    states = []
    for i in range(depth):
        j = i // 2
        if i % 2 == 0:
            if x is None:
                x_ctx, x_lat = x_prompt.reshape(t_ctx, d), x_sample.reshape(n_lat_seq * lat_len, d)
            else:
                x_ctx, x_lat = x[:t_ctx], x[t_ctx:]
            x = _conv_mixer(x_ctx, x_lat, mod4, i, norm_mix_g[i][None], sc_w_in[j].astype(BF16), sc_conv_w[j],
                            sc_w_out[j].astype(BF16), ctx_rowlen=ctx_len, **tiling(tt_mix))
        else:
            w_in = ssd_w_in[j]
            w_z = w_in[:, :d_inner].astype(BF16)
            w_x = w_in[:, d_inner:d_inner + conv_dim].astype(BF16)
            w_dt_raw = w_in[:, d_inner + conv_dim:]
            pad = jnp.zeros((d, LANES - n_heads), F32)
            w_dt = jnp.concatenate([w_dt_raw[:, :n_heads], pad, w_dt_raw[:, n_heads:], pad], axis=1).astype(BF16)
            lane_pad = ((0, 0), (0, LANES - n_heads))
            dt_bias = jnp.pad(ssd_dt_bias[j], lane_pad)
            a_log = jnp.pad(ssd_a_log[j], lane_pad)[:, None, :]
            z, xbc, dt = _ssd_in(x, mod4, i, norm_mix_g[i][None], w_z, w_x, w_dt, ssd_conv_w[j],
                                 ssd_conv_b[j][None], dt_bias, ctx_rowlen=ctx_len, **tiling(tt_mix))
            n_pairs = n_heads // 2
            h0 = state_ssm[:, j].reshape(n_lat_seq, 2, n_pairs, 2 * SSD_HEAD_DIM, SSD_STATE)
            y_f, y_b, st = _ssd_scan(xbc, dt, a_log, h0, n_heads=n_heads, n_ctx_seq=n_ctx_seq,
                                     ctx_nc=ctx_len // SSD_CHUNK, lat_nc=lat_len // SSD_CHUNK)
            states.append(st.reshape(n_ctx_seq, 2, n_heads, SSD_HEAD_DIM, SSD_STATE))
            d_skip = jnp.repeat(ssd_d[j][0] + ssd_d[j][1], SSD_HEAD_DIM)[None]
            x = _ssd_out(x, mod4, i, y_f, y_b, xbc, z, d_skip, ssd_norm_g[j][None],
                         ssd_w_out[j].astype(BF16), **tiling(tt_mix))
        keys = peer_keys[i].reshape(2 * PEER_HEADS, N_KEYS, PEER_HALF).astype(BF16)
        i1, i2, gate = _peer_route(x, mod4, i, norm_ffn_g[i][None], peer_wq[i].astype(BF16), keys,
                                   **tiling(tt_route))
        x = _peer_dense(x, mod4, i, norm_ffn_g[i][None], i1, i2, gate, u_all, v_all,
                        norm_f_g[None], keys_per_blk=16,
                        final_norm=(i == depth - 1), **tiling(tt_dense))

    y_ctx, y_lat = x
    return (y_ctx.reshape(n_ctx_seq, ctx_len, d), y_lat.reshape(n_lat_seq, lat_len, d),
            jnp.stack(states, axis=1))
```

```python
import functools
import math

import jax
import jax.numpy as jnp
from jax import lax
from jax.experimental import pallas as pl
from jax.experimental.pallas import tpu as pltpu

F32 = jnp.float32
BF16 = jnp.bfloat16
EPS = 1e-6

LANES = 128
SUBLANES = 8
VMEM_LIMIT_BYTES = 60 * 1024 * 1024

GRID_W = 64
CONV_W = 3
SSD_HEAD_DIM = 64
SSD_STATE = 128
SSD_GROUPS = 8
SSD_CHUNK = 128
N_KEYS = 128
PEER_HEADS = 8
PEER_TOPK = 16
PEER_HALF = 128

NEG_INF = float("-inf")


def _cparams(sem):
    return pltpu.CompilerParams(dimension_semantics=sem, vmem_limit_bytes=VMEM_LIMIT_BYTES)


def _norm_mod(x, g, scl, sh):
    ms = jnp.mean(x * x, axis=-1, keepdims=True)
    return x * lax.rsqrt(ms + EPS) * g * (1.0 + scl) + sh


def _silu(x):
    return x * (1.0 / (1.0 + jnp.exp(-x)))


def _mod_row_map(layer, n_ctx_tiles, tiles_per_lat_seq):
    def index_map(i, *_):
        row = jnp.where(i < n_ctx_tiles, 0, 1 + (i - n_ctx_tiles) // tiles_per_lat_seq)
        return (layer, row, 0, 0)
    return index_map


def _shift_rows(u, pos, rowlen):
    n = u.shape[0]
    up = pltpu.roll(u, 1, axis=0)
    dn = pltpu.roll(u, n - 1, axis=0)
    up = jnp.where(pos == 0, 0.0, up)
    dn = jnp.where(pos == rowlen - 1, 0.0, dn)
    return up, dn


def _adaln_kernel(c_ref, w_ref, b_ref, o_ref):
    s = _silu(c_ref[...])
    o_ref[...] = jnp.dot(s, w_ref[...], preferred_element_type=F32,
                         precision=lax.Precision.HIGHEST) + b_ref[...]


def _adaln(cond, ada_w, ada_b):
    depth, d, n = ada_w.shape
    rows = cond.shape[0]
    tn = 1536
    return pl.pallas_call(
        _adaln_kernel,
        grid=(depth, n // tn),
        in_specs=[
            pl.BlockSpec((rows, d), lambda l, j: (0, 0)),
            pl.BlockSpec((None, d, tn), lambda l, j: (l, 0, j)),
            pl.BlockSpec((None, 1, tn), lambda l, j: (l, 0, j)),
        ],
        out_specs=pl.BlockSpec((None, rows, tn), lambda l, j: (l, 0, j)),
        out_shape=jax.ShapeDtypeStruct((depth, rows, n), F32),
        compiler_params=_cparams(("arbitrary", "arbitrary")),
        name="adaln",
    )(cond, ada_w, ada_b.reshape(depth, 1, n))


def _conv_mixer_kernel(n_ctx_tiles, ctx_rowlen, xc_ref, xl_ref, mod_ref, g_ref, win_ref, cw_ref,
                       wout_ref, o_ref):
    i = pl.program_id(0)
    x = jnp.where(i < n_ctx_tiles, xc_ref[...], xl_ref[...])
    d = x.shape[1]
    mod = mod_ref[...]
    hn = _norm_mod(x, g_ref[...], mod[1:2], mod[0:1]).astype(BF16)
    p = jnp.dot(hn, win_ref[...], preferred_element_type=F32)
    bg, cg, xv = p[:, :d], p[:, d:2 * d], p[:, 2 * d:]
    u = cg * xv
    rowlen = jnp.where(i < n_ctx_tiles, ctx_rowlen, GRID_W)
    pos = lax.broadcasted_iota(jnp.int32, (x.shape[0], 1), 0) & (rowlen - 1)
    up, dn = _shift_rows(u, pos, rowlen)
    cw = cw_ref[...]
    y = up * cw[0:1] + u * cw[1:2] + dn * cw[2:3]
    mix = jnp.dot((bg * y).astype(BF16), wout_ref[...], preferred_element_type=F32)
    o_ref[...] = x + mod[2:3] * mix


def _conv_mixer(x_ctx, x_lat, mod4, layer, g, w_in, conv_w, w_out, *, tt, n_ctx_tiles, ctx_rowlen,
                tiles_per_lat_seq):
    d = x_ctx.shape[1]
    t = x_ctx.shape[0] + x_lat.shape[0]
    return pl.pallas_call(
        functools.partial(_conv_mixer_kernel, n_ctx_tiles, ctx_rowlen),
        grid=(t // tt,),
        in_specs=[
            pl.BlockSpec((tt, d), lambda i: (jnp.minimum(i, n_ctx_tiles - 1), 0)),
            pl.BlockSpec((tt, d), lambda i: (jnp.maximum(i - n_ctx_tiles, 0), 0)),
            pl.BlockSpec((None, None, 6, d), _mod_row_map(layer, n_ctx_tiles, tiles_per_lat_seq)),
            pl.BlockSpec((1, d), lambda i: (0, 0)),
            pl.BlockSpec((d, 3 * d), lambda i: (0, 0)),
            pl.BlockSpec((CONV_W, d), lambda i: (0, 0)),
            pl.BlockSpec((d, d), lambda i: (0, 0)),
        ],
        out_specs=pl.BlockSpec((tt, d), lambda i: (i, 0)),
        out_shape=jax.ShapeDtypeStruct((t, d), F32),
        compiler_params=_cparams(("arbitrary",)),
        name="conv_mixer",
    )(x_ctx, x_lat, mod4, g, w_in, conv_w, w_out)


ROUTE_TILE = SUBLANES * LANES


def _oddeven_merge_sort_pairs(n):
    pairs = []
    p = 1
    while p < n:
        k = p
        while k >= 1:
            for j in range(k % p, n - k, 2 * k):
                for i in range(min(k, n - j - k)):
                    if (i + j) // (2 * p) == (i + j + k) // (2 * p):
                        pairs.append((i + j, i + j + k))
            k //= 2
        p *= 2
    return pairs


_SORT16 = _oddeven_merge_sort_pairs(PEER_TOPK)


def _goes_first(a, ia, b, ib):
    return (a > b) | ((a == b) & (ia < ib))


def _compare_exchange(v, ix, i, j, ids_ordered=False):
    a, b, ia, ib = v[i], v[j], ix[i], ix[j]
    f = (a >= b) if ids_ordered else _goes_first(a, ia, b, ib)
    v[i], v[j] = jnp.where(f, a, b), jnp.where(f, b, a)
    ix[i], ix[j] = jnp.where(f, ia, ib), jnp.where(f, ib, ia)


def _sort16(v, ix):
    for n, (i, j) in enumerate(_SORT16):
        _compare_exchange(v, ix, i, j, ids_ordered=n < PEER_TOPK // 2)


def _bitonic_sort16(v, ix):
    d = PEER_TOPK // 2
    while d >= 1:
        for i in range(PEER_TOPK):
            if i & d == 0:
                _compare_exchange(v, ix, i, i + d)
        d //= 2


def _merge_top16(rv, ri, xv, xi, sort=True):
    for k in range(len(xv)):
        i = PEER_TOPK - 1 - k
        f = _goes_first(rv[i], ri[i], xv[k], xi[k])
        rv[i] = jnp.where(f, rv[i], xv[k])
        ri[i] = jnp.where(f, ri[i], xi[k])
    if sort:
        _bitonic_sort16(rv, ri)


def _vreg_rows(k):
    return pl.ds(k * SUBLANES, SUBLANES)


def _peer_route_kernel(x_ref, mod_ref, g_ref, wq_ref, keys_ref, i1_ref, i2_ref, gate_ref,
                       hn_s, q_s, sc_s, topv_s, topi_s, e1_s, e2_s, gt_s):
    mod = mod_ref[...]
    hn_s[...] = _norm_mod(x_ref[...], g_ref[...], mod[4:5], mod[3:4]).astype(BF16)
    for h in range(PEER_HEADS):
        qh = jnp.dot(hn_s[...], wq_ref[:, 2 * h * PEER_HALF:2 * (h + 1) * PEER_HALF],
                     preferred_element_type=F32)
        qh_t = qh.T.astype(BF16)
        q_s[2 * h] = qh_t[:PEER_HALF]
        q_s[2 * h + 1] = qh_t[PEER_HALF:]

    def stage1(hp, carry):
        st = jnp.dot(keys_ref[hp], q_s[hp], preferred_element_type=F32)
        for c in range(SUBLANES):
            sc_s[pl.ds(c, N_KEYS, stride=SUBLANES), :] = st[:, c * LANES:(c + 1) * LANES]
        rv = ri = None
        for grp in range(N_KEYS // PEER_TOPK):
            v = [sc_s[_vreg_rows(grp * PEER_TOPK + i), :] for i in range(PEER_TOPK)]
            ix = [float(grp * PEER_TOPK + i) for i in range(PEER_TOPK)]
            _sort16(v, ix)
            if rv is None:
                rv, ri = v, ix
            else:
                _merge_top16(rv, ri, v, ix)
        for k in range(PEER_TOPK):
            topv_s[hp, _vreg_rows(k), :] = rv[k]
            topi_s[hp, _vreg_rows(k), :] = ri[k]
        return carry

    lax.fori_loop(0, 2 * PEER_HEADS, stage1, 0)

    n_wide = PEER_TOPK // 2
    cand_lists = [[(a, b) for b in range(PEER_TOPK // (a + 1))] for a in range(n_wide)]
    cand_lists.append([(a, 0) for a in range(n_wide, PEER_TOPK)])

    def stage2(h, carry):
        s1 = [topv_s[2 * h, _vreg_rows(k), :] for k in range(PEER_TOPK)]
        s2 = [topv_s[2 * h + 1, _vreg_rows(k), :] for k in range(PEER_TOPK)]
        rv = rf = None
        for n, pairs in enumerate(cand_lists):
            v = [s1[a] + s2[b] for a, b in pairs]
            flat = [float(a * PEER_TOPK + b) for a, b in pairs]
            if rv is None:
                rv, rf = v, flat
            else:
                _merge_top16(rv, rf, v, flat, sort=n < len(cand_lists) - 1)
        top = s1[0] + s2[0]
        ex = [jnp.exp(v - top) for v in rv]
        denom = ex[0]
        for e in ex[1:]:
            denom = denom + e
        inv = 1.0 / denom
        k1 = [topi_s[2 * h, _vreg_rows(k), :] for k in range(PEER_TOPK)]
        k2 = [topi_s[2 * h + 1, _vreg_rows(k), :] for k in range(PEER_TOPK)]
        for k in range(PEER_TOPK):
            pos1 = jnp.floor(rf[k] * (1.0 / PEER_TOPK))
            pos2 = rf[k] - pos1 * PEER_TOPK
            e1 = jnp.zeros_like(top)
            e2 = jnp.zeros_like(top)
            for a in range(PEER_TOPK):
                e1 = jnp.where(pos1 == a, k1[a], e1)
                e2 = jnp.where(pos2 == a, k2[a], e2)
            rows = pl.ds(pl.multiple_of((h * PEER_TOPK + k) * SUBLANES, SUBLANES), SUBLANES)
            e1_s[rows, :] = e1
            e2_s[rows, :] = e2
            gt_s[rows, :] = ex[k] * inv
        return carry

    lax.fori_loop(0, PEER_HEADS, stage2, 0)

    for src, dst in ((e1_s, i1_ref), (e2_s, i2_ref), (gt_s, gate_ref)):
        for c in range(SUBLANES):
            dst[c * LANES:(c + 1) * LANES, :] = src[pl.ds(c, LANES, stride=SUBLANES), :].T


def _peer_route(x, mod4, layer, g, wq, keys, *, tt, n_ctx_tiles, tiles_per_lat_seq):
    t, d = x.shape
    nsel = PEER_HEADS * PEER_TOPK
    assert tt == ROUTE_TILE and nsel == LANES
    out = jax.ShapeDtypeStruct((t, nsel), F32)
    vreg_table = pltpu.VMEM((nsel * SUBLANES, LANES), F32)
    return pl.pallas_call(
        _peer_route_kernel,
        grid=(t // tt,),
        scratch_shapes=[
            pltpu.VMEM((tt, d), BF16),
            pltpu.VMEM((2 * PEER_HEADS, PEER_HALF, tt), BF16),
            pltpu.VMEM((N_KEYS * SUBLANES, LANES), F32),
            pltpu.VMEM((2 * PEER_HEADS, PEER_TOPK * SUBLANES, LANES), F32),
            pltpu.VMEM((2 * PEER_HEADS, PEER_TOPK * SUBLANES, LANES), F32),
            vreg_table, vreg_table, vreg_table,
        ],
        in_specs=[
            pl.BlockSpec((tt, d), lambda i: (i, 0)),
            pl.BlockSpec((None, None, 6, d), _mod_row_map(layer, n_ctx_tiles, tiles_per_lat_seq)),
            pl.BlockSpec((1, d), lambda i: (0, 0)),
            pl.BlockSpec(wq.shape, lambda i: (0, 0)),
            pl.BlockSpec(keys.shape, lambda i: (0, 0, 0)),
        ],
        out_specs=[pl.BlockSpec((tt, nsel), lambda i: (i, 0))] * 3,
        out_shape=[out, out, out],
        compiler_params=_cparams(("arbitrary",)),
        name="peer_route",
    )(x, mod4, g, wq, keys)


_HIGH_HALF = 0xFFFF0000


def _pack_bf16_pair(lo, hi):
    lo_bits = lax.bitcast_convert_type(lo.astype(BF16).astype(F32), jnp.uint32)
    hi_bits = lax.bitcast_convert_type(hi.astype(BF16).astype(F32), jnp.uint32)
    return (lo_bits >> 16) | (hi_bits & jnp.uint32(_HIGH_HALF))


def _unpack_bf16_pair(words, high):
    bits = (words & jnp.uint32(_HIGH_HALF)) if high else (words << 16)
    return lax.bitcast_convert_type(bits, F32)


def _gate_pitch(tt):
    return tt + SUBLANES


def _peer_dense_kernel(final, tt, keys_per_blk, n_ctx_tiles, x_ref, mod_ref, g_ref, i1_ref, i2_ref,
                       gate_ref, u_ref, v_ref, gf_ref, *refs):
    *out_refs, hn_s, w_s, acc_s = refs
    eb = pl.program_id(1)
    pitch = _gate_pitch(tt)
    half = N_KEYS // 2

    @pl.when(eb == 0)
    def _():
        mod = mod_ref[...]
        hn_s[...] = _norm_mod(x_ref[...], g_ref[...], mod[4:5], mod[3:4]).astype(BF16)
        acc_s[...] = jnp.zeros_like(acc_s)
        key2_ids = lax.broadcasted_iota(jnp.int32, (N_KEYS, N_KEYS), 0)
        r = key2_ids & (half - 1)
        key1_ids = (((r >> 2) << 3) + (r & 3) + jnp.where(key2_ids >= half, 4, 0)).astype(F32)
        key2_ids = key2_ids.astype(F32)

        def build(t, carry):
            i1 = i1_ref[pl.ds(t, 1), :]
            i2 = i2_ref[pl.ds(t, 1), :]
            gt = gate_ref[pl.ds(t, 1), :]
            p = jnp.where(i1 == key1_ids, gt, 0.0).astype(BF16)
            q = jnp.where(i2 == key2_ids, 1.0, 0.0).astype(BF16)
            w = lax.dot_general(p, q, (((1,), (1,)), ((), ())), preferred_element_type=F32)
            w_s[pl.ds(t, half, stride=pitch), :] = _pack_bf16_pair(w[:half], w[half:])
            return carry

        lax.fori_loop(0, tt, build, 0, unroll=64)

    act = jnp.dot(hn_s[...], u_ref[...], preferred_element_type=F32)
    hs = [None] * keys_per_blk
    for oct_ in range(keys_per_blk // SUBLANES):
        for la in range(SUBLANES // 2):
            slab = (eb * (keys_per_blk // SUBLANES) + oct_) * (SUBLANES // 2) + la
            packed = w_s[pl.ds(pl.multiple_of(slab * pitch, SUBLANES), tt), :]
            for hi in range(2):
                a = oct_ * SUBLANES + hi * (SUBLANES // 2) + la
                wa = _unpack_bf16_pair(packed, hi)
                xa = act[:, a * N_KEYS:(a + 1) * N_KEYS]
                gelu = 0.5 * xa * (1.0 + lax.erf(xa * math.sqrt(0.5)))
                hs[a] = (gelu * wa).astype(BF16)
    h = jnp.concatenate(hs, axis=1)
    acc_s[...] += jnp.dot(h, v_ref[...], preferred_element_type=F32)

    @pl.when(eb == pl.num_programs(1) - 1)
    def _():
        y = x_ref[...] + mod_ref[...][5:6] * acc_s[...]
        if not final:
            out_refs[0][...] = y
        else:
            ms = jnp.mean(y * y, axis=-1, keepdims=True)
            y = y * lax.rsqrt(ms + EPS) * gf_ref[...]
            is_ctx = pl.program_id(0) < n_ctx_tiles

            @pl.when(is_ctx)
            def _():
                out_refs[0][...] = y

            @pl.when(jnp.logical_not(is_ctx))
            def _():
                out_refs[1][...] = y


def _peer_dense(x, mod4, layer, g, i1, i2, gate, u, v, g_final, *, tt, keys_per_blk,
                n_ctx_tiles, tiles_per_lat_seq, final_norm):
    t, d = x.shape
    n_exp = v.shape[1]
    eblk = keys_per_blk * N_KEYS
    nsel = i1.shape[1]
    pitch = _gate_pitch(tt)
    mod_map = _mod_row_map(layer, n_ctx_tiles, tiles_per_lat_seq)
    if final_norm:
        t_ctx = n_ctx_tiles * tt
        out_specs = [pl.BlockSpec((tt, d), lambda i, e: (jnp.minimum(i, n_ctx_tiles - 1), 0)),
                     pl.BlockSpec((tt, d), lambda i, e: (jnp.maximum(i - n_ctx_tiles, 0), 0))]
        out_shape = [jax.ShapeDtypeStruct((t_ctx, d), F32), jax.ShapeDtypeStruct((t - t_ctx, d), F32)]
    else:
        out_specs = pl.BlockSpec((tt, d), lambda i, e: (i, 0))
        out_shape = jax.ShapeDtypeStruct((t, d), F32)
    return pl.pallas_call(
        functools.partial(_peer_dense_kernel, final_norm, tt, keys_per_blk, n_ctx_tiles),
        grid=(t // tt, n_exp // eblk),
        in_specs=[
            pl.BlockSpec((tt, d), lambda i, e: (i, 0), pipeline_mode=pl.Buffered(1)),
            pl.BlockSpec((None, None, 6, d), mod_map),
            pl.BlockSpec((1, d), lambda i, e: (0, 0)),
            pl.BlockSpec((tt, nsel), lambda i, e: (i, 0), pipeline_mode=pl.Buffered(1)),
            pl.BlockSpec((tt, nsel), lambda i, e: (i, 0), pipeline_mode=pl.Buffered(1)),
            pl.BlockSpec((tt, nsel), lambda i, e: (i, 0), pipeline_mode=pl.Buffered(1)),
            pl.BlockSpec((None, d, eblk), lambda i, e: (layer, 0, e)),
            pl.BlockSpec((None, eblk, d), lambda i, e: (layer, e, 0)),
            pl.BlockSpec((1, d), lambda i, e: (0, 0)),
        ],
        out_specs=out_specs,
        out_shape=out_shape,
        scratch_shapes=[
            pltpu.VMEM((tt, d), BF16),
            pltpu.VMEM((N_KEYS // 2 * pitch, N_KEYS), jnp.uint32),
            pltpu.VMEM((tt, d), F32),
        ],
        compiler_params=_cparams(("arbitrary", "arbitrary")),
        name="peer_dense",
    )(x, mod4, g, i1, i2, gate, u, v, g_final)


def _softplus(x):
    return jnp.maximum(x, 0.0) + jnp.log(1.0 + jnp.exp(-jnp.abs(x)))


def _ssd_in_kernel(n_ctx_tiles, ctx_rowlen, x_ref, mod_ref, g_ref, wz_ref, wx_ref, wdt_ref,
                   cw_ref, cb_ref, dtb_ref, z_ref, xbc_ref, dt_ref):
    i = pl.program_id(0)
    x = x_ref[...]
    mod = mod_ref[...]
    hn = _norm_mod(x, g_ref[...], mod[1:2], mod[0:1]).astype(BF16)
    z_ref[...] = jnp.dot(hn, wz_ref[...], preferred_element_type=F32)
    rowlen = jnp.where(i < n_ctx_tiles, ctx_rowlen, GRID_W)
    pos = lax.broadcasted_iota(jnp.int32, (x.shape[0], 1), 0) & (rowlen - 1)
    chunk = 4 * LANES
    for c0 in range(0, xbc_ref.shape[1], chunk):
        cols = slice(c0, c0 + chunk)
        xbc = jnp.dot(hn, wx_ref[:, cols], preferred_element_type=F32)
        up, dn = _shift_rows(xbc, pos, rowlen)
        cw = cw_ref[:, cols]
        xbc_ref[:, cols] = _silu(up * cw[0:1] + xbc * cw[1:2] + dn * cw[2:3] + cb_ref[:, cols])
    dt = jnp.dot(hn, wdt_ref[...], preferred_element_type=F32)
    dtb = dtb_ref[...]
    dt_ref[0] = _softplus(dt[:, :LANES] + dtb[0:1])
    dt_ref[1] = _softplus(dt[:, LANES:] + dtb[1:2])


def _ssd_in(x, mod4, layer, g, w_z, w_x, w_dt, conv_w, conv_b, dt_bias, *, tt, n_ctx_tiles,
            ctx_rowlen, tiles_per_lat_seq):
    t, d = x.shape
    dz, dx = w_z.shape[1], w_x.shape[1]
    const = lambda i: (0, 0)
    return pl.pallas_call(
        functools.partial(_ssd_in_kernel, n_ctx_tiles, ctx_rowlen),
        grid=(t // tt,),
        in_specs=[
            pl.BlockSpec((tt, d), lambda i: (i, 0)),
            pl.BlockSpec((None, None, 6, d), _mod_row_map(layer, n_ctx_tiles, tiles_per_lat_seq)),
            pl.BlockSpec((1, d), const),
            pl.BlockSpec((d, dz), const),
            pl.BlockSpec((d, dx), const),
            pl.BlockSpec((d, 2 * LANES), const),
            pl.BlockSpec((CONV_W, dx), const),
            pl.BlockSpec((1, dx), const),
            pl.BlockSpec((2, LANES), const),
        ],
        out_specs=[
            pl.BlockSpec((tt, dz), lambda i: (i, 0)),
            pl.BlockSpec((tt, dx), lambda i: (i, 0)),
            pl.BlockSpec((2, tt, LANES), lambda i: (0, i, 0)),
        ],
        out_shape=[
            jax.ShapeDtypeStruct((t, dz), F32),
            jax.ShapeDtypeStruct((t, dx), F32),
            jax.ShapeDtypeStruct((2, t, LANES), F32),
        ],
        compiler_params=_cparams(("arbitrary",)),
        name="ssd_in",
    )(x, mod4, g, w_z, w_x, w_dt, conv_w, conv_b, dt_bias)


def _ssd_scan_kernel(n_heads, ctx_chunks, ctx_nc, lat_nc, xbc_f, xbc_b, dt_f, dt_b, a_ref, e_ref,
                     h0_f, h0_b, y_f, y_b, st_f, st_b, state_f, state_b):
    step = pl.program_id(0)
    geometry = (n_heads, ctx_chunks, ctx_nc, lat_nc)
    dirs = ((True, step, xbc_f, dt_f, a_ref[0], e_ref, h0_f, y_f, st_f, state_f),
            (False, pl.num_programs(0) - 1 - step, xbc_b, dt_b, a_ref[1], e_ref, h0_b, y_b, st_b,
             state_b))
    for phase in ("init", "main", "emit"):
        for args in dirs:
            _scan_direction(phase, geometry, *args)


def _scan_direction(phase, geometry, fwd, chunk, xbc_ref, dt_ref, a_row, e_ref, h0_ref, y_ref,
                    st_ref, state_s):
    n_heads, ctx_chunks, ctx_nc, lat_nc = geometry
    q = SSD_CHUNK
    p = SSD_HEAD_DIM
    d_inner = n_heads * p
    gn = SSD_GROUPS * SSD_STATE
    heads_per_group = n_heads // SSD_GROUPS

    is_ctx = chunk < ctx_chunks
    seq_nc = jnp.where(is_ctx, ctx_nc, lat_nc)
    local = jnp.where(is_ctx, lax.rem(chunk, ctx_nc), lax.rem(chunk - ctx_chunks, lat_nc))
    seq_first = local == (0 if fwd else seq_nc - 1)
    seq_last = local == (seq_nc - 1 if fwd else 0)

    if phase == "init":
        @pl.when(jnp.logical_and(seq_first, is_ctx))
        def _():
            state_s[...] = jnp.zeros_like(state_s)

        @pl.when(jnp.logical_and(seq_first, jnp.logical_not(is_ctx)))
        def _():
            for pair in range(n_heads // 2):
                state_s[pair] = h0_ref[pair].T
        return

    if phase == "emit":
        @pl.when(jnp.logical_and(seq_last, is_ctx))
        def _():
            for pair in range(n_heads // 2):
                st_ref[pair] = state_s[pair].T
        return

    row = lax.broadcasted_iota(jnp.int32, (q, q), 0)
    col = lax.broadcasted_iota(jnp.int32, (q, q), 1)
    mask = (col <= row) if fwd else (col >= row)
    tri = jnp.where(mask, 1.0, 0.0)

    dt = dt_ref[...]
    dta = dt * (-jnp.exp(a_row))
    cum = jnp.dot(tri, dta, preferred_element_type=F32, precision=lax.Precision.HIGHEST)
    cum_t = cum.T
    tot = cum[q - 1:q, :] if fwd else cum[0:1, :]

    def per_lane(m):
        e = e_ref[...]
        hi = m.astype(BF16)
        lo = (m - hi.astype(F32)).astype(BF16)
        return jnp.dot(hi, e, preferred_element_type=F32) + jnp.dot(lo, e, preferred_element_type=F32)

    exp_cum = jnp.exp(cum)
    dt_l = per_lane(dt)
    dt_end_l = per_lane(dt * jnp.exp(tot - cum))
    exp_cum_l = per_lane(exp_cum)
    exp_tot_l = exp_cum_l[q - 1:q, :] if fwd else exp_cum_l[0:1, :]

    first = lax.broadcasted_iota(jnp.int32, (q, 2 * p), 1) < p

    for g in range(SSD_GROUPS):
        bm_f = xbc_ref[:, d_inner + g * SSD_STATE:d_inner + (g + 1) * SSD_STATE]
        bm_t = bm_f.T.astype(BF16)
        cm = xbc_ref[:, d_inner + gn + g * SSD_STATE:d_inner + gn + (g + 1) * SSD_STATE].astype(BF16)
        cb = jnp.dot(cm, bm_t, preferred_element_type=F32)
        for pr in range(heads_per_group // 2):
            hd = g * heads_per_group + 2 * pr
            pair = hd // 2
            lanes = slice(hd * p, (hd + 2) * p)
            xs = xbc_ref[:, lanes]
            xdt = xs * dt_l[:, lanes]
            ms = []
            for k in range(2):
                seg = cum[:, hd + k:hd + k + 1] - cum_t[hd + k:hd + k + 1, :]
                ms.append((jnp.where(mask, jnp.exp(seg), 0.0) * cb).astype(BF16))
            rhs = jnp.concatenate([jnp.where(first, xdt, 0.0), jnp.where(first, 0.0, xdt)],
                                  axis=0).astype(BF16)
            y = jnp.dot(jnp.concatenate(ms, axis=1), rhs, preferred_element_type=F32)
            h_prev = state_s[pair]
            y_off = jnp.dot(cm, h_prev.astype(BF16), preferred_element_type=F32)
            y_ref[:, lanes] = y + y_off * exp_cum_l[:, lanes]
            xdtw = (xs * dt_end_l[:, lanes]).astype(BF16)
            upd = jnp.dot(bm_t, xdtw, preferred_element_type=F32)
            state_s[pair] = exp_tot_l[:, lanes] * h_prev + upd


def _ssd_scan(xbc, dt, a_log, h0, *, n_heads, n_ctx_seq, ctx_nc, lat_nc):
    t, dx = xbc.shape
    d_inner = n_heads * SSD_HEAD_DIM
    n_pairs = n_heads // 2
    q = SSD_CHUNK
    n_chunks = t // q
    ctx_chunks = n_ctx_seq * ctx_nc
    state_tile = (n_pairs, 2 * SSD_HEAD_DIM, SSD_STATE)
    assert 2 * SSD_HEAD_DIM == SSD_STATE
    head_lanes = (jnp.arange(d_inner)[None, :] // SSD_HEAD_DIM == jnp.arange(LANES)[:, None]).astype(BF16)

    fwd_chunk = lambda s: s
    bwd_chunk = lambda s: n_chunks - 1 - s

    def h0_seq(chunk):
        return jnp.maximum(chunk - ctx_chunks, 0) // lat_nc

    def st_seq(chunk):
        return jnp.minimum(chunk, ctx_chunks - 1) // ctx_nc

    y_shape = jax.ShapeDtypeStruct((t, d_inner), F32)
    st_shape = jax.ShapeDtypeStruct((n_ctx_seq,) + state_tile, F32)
    y_f, y_b, st_f, st_b = pl.pallas_call(
        functools.partial(_ssd_scan_kernel, n_heads, ctx_chunks, ctx_nc, lat_nc),
        grid=(n_chunks,),
        in_specs=[
            pl.BlockSpec((q, dx), lambda s: (fwd_chunk(s), 0)),
            pl.BlockSpec((q, dx), lambda s: (bwd_chunk(s), 0)),
            pl.BlockSpec((None, q, LANES), lambda s: (0, fwd_chunk(s), 0)),
            pl.BlockSpec((None, q, LANES), lambda s: (1, bwd_chunk(s), 0)),
            pl.BlockSpec((2, 1, LANES), lambda s: (0, 0, 0)),
            pl.BlockSpec((LANES, d_inner), lambda s: (0, 0)),
            pl.BlockSpec((None, None) + state_tile, lambda s: (h0_seq(fwd_chunk(s)), 0, 0, 0, 0)),
            pl.BlockSpec((None, None) + state_tile, lambda s: (h0_seq(bwd_chunk(s)), 1, 0, 0, 0)),
        ],
        out_specs=[
            pl.BlockSpec((q, d_inner), lambda s: (fwd_chunk(s), 0)),
            pl.BlockSpec((q, d_inner), lambda s: (bwd_chunk(s), 0)),
            pl.BlockSpec((None,) + state_tile, lambda s: (st_seq(fwd_chunk(s)), 0, 0, 0)),
            pl.BlockSpec((None,) + state_tile, lambda s: (st_seq(bwd_chunk(s)), 0, 0, 0)),
        ],
        out_shape=[y_shape, y_shape, st_shape, st_shape],
        scratch_shapes=[pltpu.VMEM(state_tile, F32), pltpu.VMEM(state_tile, F32)],
        compiler_params=_cparams(("arbitrary",)),
        name="ssd_scan",
    )(xbc, xbc, dt, dt, a_log, head_lanes, h0, h0)
    return y_f, y_b, jnp.stack([st_f, st_b], axis=1)


def _ssd_out_kernel(x_ref, mod_ref, yf_ref, yb_ref, xs_ref, z_ref, dskip_ref, ng_ref, wout_ref,
                    o_ref):
    y = yf_ref[...] + yb_ref[...] + dskip_ref[...] * xs_ref[...]
    yz = y * _silu(z_ref[...])
    ms = jnp.mean(yz * yz, axis=-1, keepdims=True)
    yn = (yz * lax.rsqrt(ms + EPS) * ng_ref[...]).astype(BF16)
    mix = jnp.dot(yn, wout_ref[...], preferred_element_type=F32)
    o_ref[...] = x_ref[...] + mod_ref[...][2:3] * mix


def _ssd_out(x, mod4, layer, y_f, y_b, xbc, z, d_skip, norm_g, w_out, *, tt, n_ctx_tiles,
             tiles_per_lat_seq):
    t, d = x.shape
    di = z.shape[1]
    const = lambda i: (0, 0)
    return pl.pallas_call(
        _ssd_out_kernel,
        grid=(t // tt,),
        in_specs=[
            pl.BlockSpec((tt, d), lambda i: (i, 0)),
            pl.BlockSpec((None, None, 6, d), _mod_row_map(layer, n_ctx_tiles, tiles_per_lat_seq)),
            pl.BlockSpec((tt, di), lambda i: (i, 0)),
            pl.BlockSpec((tt, di), lambda i: (i, 0)),
            pl.BlockSpec((tt, di), lambda i: (i, 0)),
            pl.BlockSpec((tt, di), lambda i: (i, 0)),
            pl.BlockSpec((1, di), const),
            pl.BlockSpec((1, di), const),
            pl.BlockSpec((di, d), const),
        ],
        out_specs=pl.BlockSpec((tt, d), lambda i: (i, 0)),
        out_shape=jax.ShapeDtypeStruct((t, d), F32),
        compiler_params=_cparams(("arbitrary",)),
        name="ssd_out",
    )(x, mod4, y_f, y_b, xbc, z, d_skip, norm_g, w_out)


def _pick_tile(candidates, *lengths):
    for tt in candidates:
        if all(n % tt == 0 for n in lengths):
            return tt
    raise ValueError(f"no tile in {candidates} divides {lengths}")


def kernel(x_prompt, x_sample, state_ssm, c, c_ctx, norm_mix_g, norm_ffn_g, norm_f_g, ada_w, ada_b, sc_w_in, sc_conv_w, sc_w_out, ssd_w_in, ssd_conv_w, ssd_conv_b, ssd_dt_bias, ssd_a_log, ssd_d, ssd_norm_g, ssd_w_out, peer_wq, peer_keys, peer_u, peer_v):
    n_ctx_seq, ctx_len, d = x_prompt.shape
    n_lat_seq, lat_len, _ = x_sample.shape
    depth = ada_w.shape[0]
    t_ctx = n_ctx_seq * ctx_len
    d_inner = ssd_norm_g.shape[1]
    n_heads = d_inner // SSD_HEAD_DIM
    conv_dim = ssd_conv_w.shape[2]
    assert ctx_len & (ctx_len - 1) == 0 and ctx_len % SSD_CHUNK == 0 and lat_len % SSD_CHUNK == 0
    assert n_heads % (2 * SSD_GROUPS) == 0 and n_heads <= LANES

    x = None
    n_cond = 1 + n_lat_seq
    cond_rows = -(-n_cond // SUBLANES) * SUBLANES
    cond = jnp.concatenate([c_ctx[None], c, jnp.zeros((cond_rows - n_cond, d), F32)], axis=0)
    mod4 = _adaln(cond, ada_w, ada_b).reshape(depth, cond_rows, 6, d)

    def tiling(tt):
        return dict(tt=tt, n_ctx_tiles=t_ctx // tt, tiles_per_lat_seq=lat_len // tt)

    tt_mix = _pick_tile((512, 256), t_ctx, lat_len)
    tt_route = _pick_tile((ROUTE_TILE,), t_ctx, lat_len)
    tt_dense = _pick_tile((512, 256), t_ctx, lat_len)
    assert tt_mix % ctx_len == 0 and tt_mix % GRID_W == 0

    u_all, v_all = peer_u.astype(BF16).transpose(0, 2, 1), peer_v.astype(BF16)
    states = []
    for i in range(depth):
        j = i // 2
        if i % 2 == 0:
            if x is None:
                x_ctx, x_lat = x_prompt.reshape(t_ctx, d), x_sample.reshape(n_lat_seq * lat_len, d)
            else:
                x_ctx, x_lat = x[:t_ctx], x[t_ctx:]
            x = _conv_mixer(x_ctx, x_lat, mod4, i, norm_mix_g[i][None], sc_w_in[j].astype(BF16), sc_conv_w[j],
                            sc_w_out[j].astype(BF16), ctx_rowlen=ctx_len, **tiling(tt_mix))
        else:
            w_in = ssd_w_in[j]
            w_z = w_in[:, :d_inner].astype(BF16)
            w_x = w_in[:, d_inner:d_inner + conv_dim].astype(BF16)
            w_dt_raw = w_in[:, d_inner + conv_dim:]
            pad = jnp.zeros((d, LANES - n_heads), F32)
            w_dt = jnp.concatenate([w_dt_raw[:, :n_heads], pad, w_dt_raw[:, n_heads:], pad], axis=1).astype(BF16)
            lane_pad = ((0, 0), (0, LANES - n_heads))
            dt_bias = jnp.pad(ssd_dt_bias[j], lane_pad)
            a_log = jnp.pad(ssd_a_log[j], lane_pad)[:, None, :]
            z, xbc, dt = _ssd_in(x, mod4, i, norm_mix_g[i][None], w_z, w_x, w_dt, ssd_conv_w[j],
                                 ssd_conv_b[j][None], dt_bias, ctx_rowlen=ctx_len, **tiling(tt_mix))
            n_pairs = n_heads // 2
            h0 = state_ssm[:, j].reshape(n_lat_seq, 2, n_pairs, 2 * SSD_HEAD_DIM, SSD_STATE)
            y_f, y_b, st = _ssd_scan(xbc, dt, a_log, h0, n_heads=n_heads, n_ctx_seq=n_ctx_seq,
                                     ctx_nc=ctx_len // SSD_CHUNK, lat_nc=lat_len // SSD_CHUNK)
            states.append(st.reshape(n_ctx_seq, 2, n_heads, SSD_HEAD_DIM, SSD_STATE))
            d_skip = jnp.repeat(ssd_d[j][0] + ssd_d[j][1], SSD_HEAD_DIM)[None]
            x = _ssd_out(x, mod4, i, y_f, y_b, xbc, z, d_skip, ssd_norm_g[j][None],
                         ssd_w_out[j].astype(BF16), **tiling(tt_mix))
        keys = peer_keys[i].reshape(2 * PEER_HEADS, N_KEYS, PEER_HALF).astype(BF16)
        i1, i2, gate = _peer_route(x, mod4, i, norm_ffn_g[i][None], peer_wq[i].astype(BF16), keys,
                                   **tiling(tt_route))
        x = _peer_dense(x, mod4, i, norm_ffn_g[i][None], i1, i2, gate, u_all, v_all,
                        norm_f_g[None], keys_per_blk=16,
                        final_norm=(i == depth - 1), **tiling(tt_dense))

    y_ctx, y_lat = x
    return (y_ctx.reshape(n_ctx_seq, ctx_len, d), y_lat.reshape(n_lat_seq, lat_len, d),
            jnp.stack(states, axis=1))
```
